```python
import math
import jax, jax.numpy as jnp
from jax import lax
import numpy as np

D_MODEL = 1024
BATCH = 4
SEQ = 4096
DEPTH = 2

EXPAND = 2
MIX_W = EXPAND * D_MODEL
POOL_W = MIX_W // 2
POOL_WINDOWS = (2, 4, 8, 16)
N_POOL_GROUPS = len(POOL_WINDOWS)
POOL_G = POOL_W // N_POOL_GROUPS
DN_W = MIX_W - POOL_W
HEAD_DIM = 128
N_HEADS = DN_W // HEAD_DIM
N_DIRS = 2
CONV_K = 5
CHUNK = 64
EPS = 1e-6
OFF_UP = 0
OFF_ZP = OFF_UP + POOL_W
OFF_QKV = OFF_ZP + POOL_W
OFF_ZD = OFF_QKV + 3 * DN_W
OFF_B = OFF_ZD + DN_W
OFF_A = OFF_B + N_DIRS * N_HEADS
IN_W = OFF_A + N_DIRS * N_HEADS

kernel_name = "hybrid_pool_gated_deltanet_bidir"


def rmsnorm(x, w):
    xf = x.astype(jnp.float32)
    y = xf * lax.rsqrt(jnp.mean(xf * xf, axis=-1, keepdims=True) + EPS)
    return (y * w.astype(jnp.float32)).astype(x.dtype)


def l2norm(x):
    return x * lax.rsqrt(jnp.sum(x * x, axis=-1, keepdims=True) + EPS)


def pool_mixer(u, pool_w, pool_scale):
    b, s, _ = u.shape
    uf = u.astype(jnp.float32)
    cs = jnp.concatenate([jnp.zeros((b, 1, POOL_W), jnp.float32), jnp.cumsum(uf, axis=1)], axis=1)
    t = np.arange(s)
    outs = []
    for gi, w in enumerate(POOL_WINDOWS):
        left = w // 2
        right = w - 1 - left
        lo = np.maximum(t - left, 0)
        hi = np.minimum(t + right, s - 1)
        cnt = jnp.asarray((hi - lo + 1).astype(np.float32))
        csg = cs[..., gi * POOL_G:(gi + 1) * POOL_G]
        mean = (csg[:, hi + 1] - csg[:, lo]) / cnt[None, :, None]
        outs.append(mean - uf[..., gi * POOL_G:(gi + 1) * POOL_G])
    m = jnp.stack(outs, axis=2).astype(u.dtype)
    y = jnp.einsum('bsgc,gcd->bsgd', m, pool_w).reshape(b, s, POOL_W)
    return y * pool_scale


def gated_delta_chunked(q, k, v, g, beta):
    b, s, h, dk = q.shape
    dv = v.shape[-1]
    n = s // CHUNK
    q = l2norm(q) * (dk ** -0.5)
    k = l2norm(k)

    def blk(t):
        return t.reshape(b, n, CHUNK, h, -1).transpose(0, 3, 1, 2, 4)

    q, k, v = blk(q), blk(k), blk(v)
    g = g.reshape(b, n, CHUNK, h).transpose(0, 3, 1, 2)
    beta = beta.reshape(b, n, CHUNK, h).transpose(0, 3, 1, 2)
    gc = jnp.cumsum(g, axis=-1)
    tril = jnp.tril(jnp.ones((CHUNK, CHUNK), bool))
    strict = jnp.tril(jnp.ones((CHUNK, CHUNK), bool), -1)
    diff = gc[..., :, None] - gc[..., None, :]
    decay = jnp.where(tril, jnp.exp(jnp.where(tril, diff, 0.0)), 0.0)
    kb = k * beta[..., None]
    lmat = jnp.where(strict, jnp.einsum('bhncd,bhnsd->bhncs', kb, k) * decay, 0.0)
    eye = jnp.eye(CHUNK, dtype=jnp.float32)
    tinv = lax.linalg.triangular_solve(eye + lmat, jnp.broadcast_to(eye, lmat.shape),
                                       left_side=True, lower=True, unit_diagonal=True)
    u = jnp.einsum('bhncs,bhnsd->bhncd', tinv, v * beta[..., None])
    w = jnp.einsum('bhncs,bhnsd->bhncd', tinv, kb * jnp.exp(gc)[..., None])
    a_intra = jnp.einsum('bhncd,bhnsd->bhncs', q, k) * decay

    def step(state, inp):
        q_c, k_c, u_c, w_c, gc_c, a_c = inp
        v_new = u_c - jnp.einsum('bhcd,bhde->bhce', w_c, state)
        o = (jnp.einsum('bhcd,bhde->bhce', q_c * jnp.exp(gc_c)[..., None], state)
             + jnp.einsum('bhcs,bhse->bhce', a_c, v_new))
        g_last = gc_c[..., -1]
        state = (state * jnp.exp(g_last)[..., None, None]
                 + jnp.einsum('bhcd,bhce->bhde', k_c * jnp.exp(g_last[..., None] - gc_c)[..., None], v_new))
        return state, o

    mv = lambda t: jnp.moveaxis(t, 2, 0)
    xs = (mv(q), mv(k), mv(u), mv(w), mv(gc), mv(a_intra))
    s0 = jnp.zeros((b, h, dk, dv), jnp.float32)
    _, o = lax.scan(step, s0, xs)
    return o.transpose(1, 0, 3, 2, 4).reshape(b, s, h, dv)


def deltanet_mixer(qkv, b_logit, a_logit, conv_w, a_log, dt_bias, head_norm_w, z):
    bsz, s, _ = qkv.shape
    c = 3 * DN_W
    qkv = lax.conv_general_dilated(qkv, conv_w.astype(qkv.dtype).reshape(CONV_K, 1, c), window_strides=(1,),
                                   padding=[(CONV_K // 2, CONV_K // 2)],
                                   dimension_numbers=('NWC', 'WIO', 'NWC'), feature_group_count=c)
    qkv = jax.nn.silu(qkv).astype(jnp.float32)
    q, k, v = jnp.split(qkv, 3, axis=-1)
    q = q.reshape(bsz, s, N_HEADS, HEAD_DIM)
    k = k.reshape(bsz, s, N_HEADS, HEAD_DIM)
    v = v.reshape(bsz, s, N_HEADS, HEAD_DIM)
    beta = jax.nn.sigmoid(b_logit.astype(jnp.float32))
    g = -jnp.exp(a_log.astype(jnp.float32)) * jax.nn.softplus(a_logit.astype(jnp.float32) + dt_bias.astype(jnp.float32))
    o_f = gated_delta_chunked(q, k, v, g[:, :, 0], beta[:, :, 0])
    fl = lambda t: jnp.flip(t, axis=1)
    o_b = fl(gated_delta_chunked(fl(q), fl(k), fl(v), fl(g[:, :, 1]), fl(beta[:, :, 1])))
    o = (o_f + o_b).astype(z.dtype)
    o = rmsnorm(o, head_norm_w).reshape(bsz, s, DN_W)
    return o * jax.nn.silu(z)


def setup_inputs(seed: int = 0) -> dict:
    key = jax.random.key(seed)
    ks = jax.random.split(key, 12)
    f32 = jnp.float32
    x = jax.random.normal(ks[0], (BATCH, SEQ, D_MODEL), f32)
    norm_w = 1.0 + 0.1 * jax.random.normal(ks[1], (DEPTH, D_MODEL), f32)
    w_in = jax.random.normal(ks[2], (DEPTH, D_MODEL, IN_W), f32) * D_MODEL ** -0.5
    pool_w = jax.random.normal(ks[3], (DEPTH, N_POOL_GROUPS, POOL_G, POOL_G), f32) * POOL_G ** -0.5
    pool_scale = 1.0 + 0.1 * jax.random.normal(ks[4], (DEPTH, POOL_W), f32)
    conv_w = jax.random.normal(ks[5], (DEPTH, CONV_K, 3 * DN_W), f32) * CONV_K ** -0.5
    a_log = jnp.log(jax.random.uniform(ks[6], (DEPTH, N_DIRS, N_HEADS), f32, 1.0, 16.0))
    dt = jnp.exp(jax.random.uniform(ks[7], (DEPTH, N_DIRS, N_HEADS), f32, math.log(1e-3), math.log(1e-1)))
    dt_bias = dt + jnp.log(-jnp.expm1(-dt))
    head_norm_w = 1.0 + 0.1 * jax.random.normal(ks[8], (DEPTH, HEAD_DIM), f32)
    w_out = jax.random.normal(ks[9], (DEPTH, MIX_W, D_MODEL), f32) * (MIX_W ** -0.5) * 0.5
    final_norm_w = 1.0 + 0.1 * jax.random.normal(ks[10], (D_MODEL,), f32)
    return {"x": x, "norm_w": norm_w, "w_in": w_in, "pool_w": pool_w, "pool_scale": pool_scale,
            "conv_w": conv_w, "a_log": a_log, "dt_bias": dt_bias, "head_norm_w": head_norm_w,
            "w_out": w_out, "final_norm_w": final_norm_w}


def reference(x, norm_w, w_in, pool_w, pool_scale, conv_w, a_log, dt_bias, head_norm_w, w_out, final_norm_w):
    bsz, s, _ = x.shape
    for l in range(DEPTH):
        h = rmsnorm(x, norm_w[l])
        proj = jnp.einsum('bsd,de->bse', h, w_in[l])
        u_p = proj[..., OFF_UP:OFF_ZP]
        z_p = proj[..., OFF_ZP:OFF_QKV]
        qkv = proj[..., OFF_QKV:OFF_ZD]
        z_d = proj[..., OFF_ZD:OFF_B]
        b_logit = proj[..., OFF_B:OFF_A].reshape(bsz, s, N_DIRS, N_HEADS)
        a_logit = proj[..., OFF_A:IN_W].reshape(bsz, s, N_DIRS, N_HEADS)
        y_pool = pool_mixer(u_p, pool_w[l], pool_scale[l]) * jax.nn.silu(z_p)
        y_dn = deltanet_mixer(qkv, b_logit, a_logit, conv_w[l], a_log[l], dt_bias[l], head_norm_w[l], z_d)
        y = jnp.einsum('bse,ed->bsd', jnp.concatenate([y_pool, y_dn], axis=-1), w_out[l])
        x = x + y
    return rmsnorm(x, final_norm_w)
```

```python
import functools

import jax
import jax.numpy as jnp
from jax import lax
from jax.experimental import pallas as pl
from jax.experimental.pallas import tpu as pltpu

F32 = jnp.float32
BF16 = jnp.bfloat16

EPS = 1e-6
POOL_WINDOWS = (2, 4, 8, 16)
N_POOL_GROUPS = len(POOL_WINDOWS)
HEAD_DIM = 128
CONV_K = 5
CHUNK = 64
GROUP = 256
CHUNKS_PER_GROUP = GROUP // CHUNK
LANES = 128
MIB = 1024 * 1024


def _silu(x):
    return x * jax.nn.sigmoid(x)


def _dot(a, b):
    return jnp.dot(a, b, preferred_element_type=F32)


def _inproj_body(x_ref, nw_ref, w_ref, wg_ref, o_ref, g_ref, h_scr, *, col_tile):
    x = x_ref[...]
    ms = jnp.mean(x * x, axis=-1, keepdims=True)
    h_scr[...] = (x * lax.rsqrt(ms + EPS) * nw_ref[...]).astype(BF16)
    g_ref[...] = _dot(h_scr[...], wg_ref[...])
    for n in range(w_ref.shape[1] // col_tile):
        cols = slice(n * col_tile, (n + 1) * col_tile)
        o_ref[:, cols] = _dot(h_scr[...], w_ref[:, cols]).astype(BF16)


def _inproj(x2, norm_w, w_main, w_gate, *, row_tile=512, col_tile=512):
    m, d = x2.shape
    n_main = w_main.shape[1]
    return pl.pallas_call(
        functools.partial(_inproj_body, col_tile=col_tile),
        grid=(m // row_tile,),
        in_specs=[
            pl.BlockSpec((row_tile, d), lambda i: (i, 0)),
            pl.BlockSpec((1, d), lambda i: (0, 0)),
            pl.BlockSpec((d, n_main), lambda i: (0, 0), pipeline_mode=pl.Buffered(1)),
            pl.BlockSpec((d, LANES), lambda i: (0, 0)),
        ],
        out_specs=[
            pl.BlockSpec((row_tile, n_main), lambda i: (i, 0)),
            pl.BlockSpec((row_tile, LANES), lambda i: (i, 0)),
        ],
        out_shape=[
            jax.ShapeDtypeStruct((m, n_main), BF16),
            jax.ShapeDtypeStruct((m, LANES), F32),
        ],
        scratch_shapes=[pltpu.VMEM((row_tile, d), BF16)],
        compiler_params=pltpu.CompilerParams(
            dimension_semantics=("arbitrary",), vmem_limit_bytes=48 * MIB),
        name="inproj",
    )(x2, norm_w, w_main, w_gate)


POOL_TILE = 256
POOL_HALO = 16


def _pool_body(u_ref, z_ref, pw_ref, ps_ref, o_ref):
    seq = u_ref.shape[0]
    g = pl.program_id(1)
    left = jnp.left_shift(jnp.int32(1), g)
    right = left - 1
    i = lax.broadcasted_iota(jnp.int32, (POOL_TILE, POOL_TILE), 0)
    j = lax.broadcasted_iota(jnp.int32, (POOL_TILE, POOL_TILE), 1)
    band = ((j >= i - left) & (j <= i + right)).astype(BF16)
    ih = lax.broadcasted_iota(jnp.int32, (POOL_TILE, POOL_HALO), 0)
    ph = lax.broadcasted_iota(jnp.int32, (POOL_TILE, POOL_HALO), 1)
    band_prev = (ph - POOL_HALO >= ih - left).astype(BF16)
    band_next = (ph + POOL_TILE <= ih + right).astype(BF16)
    pw = pw_ref[...]
    ps = ps_ref[...]
    tok0 = lax.broadcasted_iota(jnp.int32, (POOL_TILE, u_ref.shape[1]), 0)
    n_tiles = seq // POOL_TILE
    for t in range(n_tiles):
        t0 = t * POOL_TILE
        u = u_ref[t0:t0 + POOL_TILE, :]
        wsum = _dot(band, u)
        if t > 0:
            wsum += _dot(band_prev, u_ref[t0 - POOL_HALO:t0, :])
        if t < n_tiles - 1:
            wsum += _dot(band_next, u_ref[t0 + POOL_TILE:t0 + POOL_TILE + POOL_HALO, :])
        tok = tok0 + t0
        cnt = jnp.minimum(tok + right, seq - 1) - jnp.maximum(tok - left, 0) + 1
        m = wsum / cnt.astype(F32) - u.astype(F32)
        y = _dot(m.astype(BF16), pw) * ps
        o_ref[t0:t0 + POOL_TILE, :] = (y * _silu(z_ref[t0:t0 + POOL_TILE, :].astype(F32))).astype(BF16)


def _pool(proj, pool_w, pool_scale, *, batch, seq):
    pool_g = pool_w.shape[-1]
    return pl.pallas_call(
        _pool_body,
        grid=(batch, N_POOL_GROUPS),
        in_specs=[
            pl.BlockSpec((seq, pool_g), lambda b, g: (b, g)),
            pl.BlockSpec((seq, pool_g), lambda b, g: (b, N_POOL_GROUPS + g)),
            pl.BlockSpec((None, pool_g, pool_g), lambda b, g: (g, 0, 0)),
            pl.BlockSpec((None, 1, pool_g), lambda b, g: (g, 0, 0)),
        ],
        out_specs=pl.BlockSpec((seq, pool_g), lambda b, g: (b, g)),
        out_shape=jax.ShapeDtypeStruct((batch * seq, N_POOL_GROUPS * pool_g), BF16),
        compiler_params=pltpu.CompilerParams(
            dimension_semantics=("arbitrary", "arbitrary"), vmem_limit_bytes=40 * MIB),
        name="pool",
    )(proj, proj, pool_w, pool_scale)


CONV_TILE = 512
CONV_PAD = 16


def _seg_scan(x, pos, *, suffix):
    n = x.shape[1]
    s = 1
    while s < CHUNK:
        if suffix:
            x = x + jnp.where(pos < CHUNK - s, pltpu.roll(x, n - s, axis=1), 0.0)
        else:
            x = x + jnp.where(pos >= s, pltpu.roll(x, s, axis=1), 0.0)
        s *= 2
    return x


def _dn_body(alog_ref, dtb_ref, q_ref, k_ref, v_ref, z_ref, gl_ref, cwq_ref, cwk_ref, cwv_ref, hnw_ref,
             o_ref,
             xpad, qh, kh, vh, kt, gc_row, er_row, rep, u_scr, w_scr, ak_scr, s_scr, o_acc):
    seq = q_ref.shape[0]
    n_chunks = seq // CHUNK
    h = pl.program_id(1)

    xpad[0:CONV_PAD, :] = jnp.zeros((CONV_PAD, HEAD_DIM), F32)
    xpad[CONV_PAD + seq:CONV_PAD + seq + CONV_PAD, :] = jnp.zeros((CONV_PAD, HEAD_DIM), F32)
    for which, (src, cw_ref) in enumerate(((q_ref, cwq_ref), (k_ref, cwk_ref), (v_ref, cwv_ref))):
        for t in range(seq // CONV_TILE):
            t0 = t * CONV_TILE
            xpad[CONV_PAD + t0:CONV_PAD + t0 + CONV_TILE, :] = src[t0:t0 + CONV_TILE, :].astype(F32)
        for t in range(seq // CONV_TILE):
            t0 = t * CONV_TILE
            acc = jnp.zeros((CONV_TILE, HEAD_DIM), F32)
            for tap in range(CONV_K):
                lo = CONV_PAD + t0 + tap - CONV_K // 2
                acc = acc + cw_ref[tap:tap + 1, :] * xpad[lo:lo + CONV_TILE, :]
            s = _silu(acc)
            rows = slice(t0, t0 + CONV_TILE)
            if which == 0:
                s = s * lax.rsqrt(jnp.sum(s * s, axis=-1, keepdims=True) + EPS) * (HEAD_DIM ** -0.5)
                qh[rows, :] = s.astype(BF16)
            elif which == 1:
                s = s * lax.rsqrt(jnp.sum(s * s, axis=-1, keepdims=True) + EPS)
                kh[rows, :] = s.astype(BF16)
                kt[:, rows] = s.T.astype(BF16)
            else:
                vh[rows, :] = s.astype(BF16)

    r = gl_ref[...]
    rowi = lax.broadcasted_iota(jnp.int32, r.shape, 0)
    pos = lax.broadcasted_iota(jnp.int32, r.shape, 1) & (CHUNK - 1)
    fwd_row = rowi == 0
    dtb = jnp.where(fwd_row, dtb_ref[0, h], dtb_ref[1, h])
    a_coef = jnp.exp(jnp.where(fwd_row, alog_ref[0, h], alog_ref[1, h]))
    g = jnp.where(rowi < 2, -a_coef * jax.nn.softplus(r + dtb), 0.0)
    pre = _seg_scan(g, pos, suffix=False)
    suf = _seg_scan(g, pos, suffix=True)
    gc = jnp.where(fwd_row, pre, suf)
    tot = pre + suf - g
    gc_row[...] = gc
    er_row[...] = jnp.exp(tot - gc)
    beta = jax.nn.sigmoid(r)
    for row, src in enumerate((gc, gc, beta, beta)):
        for t in range(seq // LANES):
            tile = jnp.broadcast_to(src[row:row + 1, t * LANES:(t + 1) * LANES], (LANES, LANES))
            rep[row, t * LANES:(t + 1) * LANES, :] = tile.T

    ii = lax.broadcasted_iota(jnp.int32, (GROUP, GROUP), 0)
    jj = lax.broadcasted_iota(jnp.int32, (GROUP, GROUP), 1)
    same = (ii // CHUNK) == (jj // CHUNK)
    eye = (ii == jj).astype(F32)

    def group_step(gi, carry):
        r0 = pl.multiple_of(gi * GROUP, GROUP)
        rows = pl.ds(r0, GROUP)
        kt_g = kt[:, rows]
        kh_g = kh[rows, :]
        kk = _dot(kh_g, kt_g)
        qk = _dot(qh[rows, :], kt_g)
        kf = kh_g.astype(F32)
        vf = vh[rows, :].astype(F32)
        for d in range(2):
            incl = same & ((jj <= ii) if d == 0 else (jj >= ii))
            strict = same & ((jj < ii) if d == 0 else (jj > ii))
            gc_c = rep[d, rows, :]
            beta_c = rep[2 + d, rows, :]
            gc_c2 = jnp.concatenate([gc_c, gc_c], axis=1)
            beta_c2 = jnp.concatenate([beta_c, beta_c], axis=1)
            diff = gc_c2 - gc_row[d:d + 1, rows]
            decay = jnp.where(incl, jnp.exp(jnp.where(incl, diff, 0.0)), 0.0)
            mk = jnp.where(strict, -(beta_c2 * kk * decay), 0.0)
            tinv = eye + mk
            mb = mk.astype(BF16)
            mk = _dot(mb, mb)
            p = 2
            while p < CHUNK // 2:
                mb = mk.astype(BF16)
                x2 = _dot(jnp.concatenate([tinv.astype(BF16), mb], axis=0), mb)
                tinv = tinv + x2[:GROUP]
                mk = x2[GROUP:]
                p *= 2
            tinv = tinv + _dot(tinv.astype(BF16), mk.astype(BF16))
            egc = jnp.exp(gc_c)
            rhs = jnp.concatenate([vf * beta_c, kf * (beta_c * egc)], axis=1).astype(BF16)
            uw = _dot(tinv.astype(BF16), rhs)
            u_scr[d, rows, :] = uw[:, :HEAD_DIM]
            w_scr[d, rows, :] = uw[:, HEAD_DIM:].astype(BF16)
            a = jnp.where(incl, qk * decay, 0.0)
            a = a[:, :LANES] + a[:, LANES:]
            a = a + pltpu.roll(a, CHUNK, axis=1)
            kt_til = kt_g.astype(F32) * er_row[d:d + 1, rows]
            for c in range(CHUNKS_PER_GROUP):
                ci = gi * CHUNKS_PER_GROUP + c
                ak_scr[d, ci, 0:CHUNK, :] = a[c * CHUNK:(c + 1) * CHUNK, :CHUNK].astype(BF16)
                half = kt_til[:, (c // 2) * LANES:(c // 2 + 1) * LANES]
                if c % 2:
                    half = pltpu.roll(half, CHUNK, axis=1)
                ak_scr[d, ci, CHUNK:, :] = half[:, :CHUNK].astype(BF16)
        return carry

    lax.fori_loop(0, seq // GROUP, group_step, 0)

    s_scr[...] = jnp.zeros(s_scr.shape, F32)
    o_acc[...] = jnp.zeros(o_acc.shape, F32)

    def chunk_step(t, carry):
        for d in range(2):
            c = t if d == 0 else n_chunks - 1 - t
            r0 = pl.multiple_of(c * CHUNK, CHUNK)
            rows = pl.ds(r0, CHUNK)
            state = s_scr[d]
            lhs = jnp.concatenate([w_scr[d, rows, :], qh[rows, :]], axis=0)
            r1 = _dot(lhs, state.astype(BF16))
            v_new = u_scr[d, rows, :] - r1[:CHUNK]
            r2 = _dot(ak_scr[d, c], v_new.astype(BF16))
            gc_c = rep[d, rows, :]
            o_acc[rows, :] += r1[CHUNK:] * jnp.exp(gc_c) + r2[:CHUNK]
            last = r0 + (CHUNK - 1 if d == 0 else 0)
            dec = jnp.exp(rep[d, pl.ds(last, 1), :])
            s_scr[d] = state * dec + r2[CHUNK:]
        return carry

    lax.fori_loop(0, n_chunks, chunk_step, 0)

    hnw = hnw_ref[...]
    for t in range(seq // CONV_TILE):
        rows = slice(t * CONV_TILE, (t + 1) * CONV_TILE)
        o = o_acc[rows, :]
        y = o * lax.rsqrt(jnp.mean(o * o, axis=-1, keepdims=True) + EPS) * hnw
        o_ref[rows, :] = (y * _silu(z_ref[rows, :].astype(F32))).astype(BF16)


def _deltanet(proj, gate_rows, conv_w3, a_log, dt_bias, head_norm_w, *, batch, seq, n_heads, qkv_block0, z_block0):
    n_chunks = seq // CHUNK
    smem = pl.BlockSpec(memory_space=pltpu.SMEM)
    head_cols = lambda off: pl.BlockSpec((seq, HEAD_DIM), lambda b, h: (b, off + h))
    conv_cols = lambda off: pl.BlockSpec((None, 8, HEAD_DIM), lambda b, h: (off + h, 0, 0))
    return pl.pallas_call(
        _dn_body,
        grid=(batch, n_heads),
        in_specs=[
            smem, smem,
            head_cols(qkv_block0), head_cols(qkv_block0 + n_heads), head_cols(qkv_block0 + 2 * n_heads),
            head_cols(z_block0),
            pl.BlockSpec((None, None, 8, seq), lambda b, h: (b, h, 0, 0)),
            conv_cols(0), conv_cols(n_heads), conv_cols(2 * n_heads),
            pl.BlockSpec((1, HEAD_DIM), lambda b, h: (0, 0)),
        ],
        out_specs=pl.BlockSpec((seq, HEAD_DIM), lambda b, h: (b, h)),
        out_shape=jax.ShapeDtypeStruct((batch * seq, n_heads * HEAD_DIM), BF16),
        scratch_shapes=[
            pltpu.VMEM((seq + 2 * CONV_PAD, HEAD_DIM), F32),
            pltpu.VMEM((seq, HEAD_DIM), BF16),
            pltpu.VMEM((seq, HEAD_DIM), BF16),
            pltpu.VMEM((seq, HEAD_DIM), BF16),
            pltpu.VMEM((HEAD_DIM, seq), BF16),
            pltpu.VMEM((8, seq), F32),
            pltpu.VMEM((8, seq), F32),
            pltpu.VMEM((4, seq, LANES), F32),
            pltpu.VMEM((2, seq, HEAD_DIM), F32),
            pltpu.VMEM((2, seq, HEAD_DIM), BF16),
            pltpu.VMEM((2, n_chunks, CHUNK + HEAD_DIM, CHUNK), BF16),
            pltpu.VMEM((2, HEAD_DIM, HEAD_DIM), F32),
            pltpu.VMEM((seq, HEAD_DIM), F32),
        ],
        compiler_params=pltpu.CompilerParams(
            dimension_semantics=("arbitrary", "arbitrary"), vmem_limit_bytes=56 * MIB),
        name="deltanet",
    )(a_log, dt_bias, proj, proj, proj, proj, gate_rows, conv_w3, conv_w3, conv_w3, head_norm_w)


def _outproj_body(yp_ref, yd_ref, x_ref, w1_ref, w2_ref, fnw_ref, o_ref, *, final):
    acc = x_ref[...] + _dot(yp_ref[...], w1_ref[...]) + _dot(yd_ref[...], w2_ref[...])
    if final:
        acc = acc * lax.rsqrt(jnp.mean(acc * acc, axis=-1, keepdims=True) + EPS) * fnw_ref[...]
    o_ref[...] = acc


def _outproj(y_pool, y_dn, x2, w1, w2, final_norm_w, *, final, row_tile=512):
    m, d = x2.shape
    k1, k2 = w1.shape[0], w2.shape[0]
    return pl.pallas_call(
        functools.partial(_outproj_body, final=final),
        grid=(m // row_tile,),
        in_specs=[
            pl.BlockSpec((row_tile, k1), lambda i: (i, 0)),
            pl.BlockSpec((row_tile, k2), lambda i: (i, 0)),
            pl.BlockSpec((row_tile, d), lambda i: (i, 0)),
            pl.BlockSpec((k1, d), lambda i: (0, 0)),
            pl.BlockSpec((k2, d), lambda i: (0, 0)),
            pl.BlockSpec((1, d), lambda i: (0, 0)),
        ],
        out_specs=pl.BlockSpec((row_tile, d), lambda i: (i, 0)),
        out_shape=jax.ShapeDtypeStruct((m, d), F32),
        compiler_params=pltpu.CompilerParams(
            dimension_semantics=("arbitrary",), vmem_limit_bytes=40 * MIB),
        name="outproj",
    )(y_pool, y_dn, x2, w1, w2, final_norm_w)


def kernel(x, norm_w, w_in, pool_w, pool_scale, conv_w, a_log, dt_bias, head_norm_w, w_out, final_norm_w):
    batch, seq, d_model = x.shape
    depth = norm_w.shape[0]
    pool_g = pool_w.shape[-1]
    pool_width = N_POOL_GROUPS * pool_g
    n_dirs, n_heads = a_log.shape[1], a_log.shape[2]
    dn_width = n_heads * HEAD_DIM
    n_main = 2 * pool_width + 4 * dn_width
    n_gate = 2 * n_dirs * n_heads
    qkv_block0 = 2 * pool_width // HEAD_DIM
    z_block0 = qkv_block0 + 3 * n_heads

    x2 = x.reshape(batch * seq, d_model)
    for l in range(depth):
        w_main = w_in[l, :, :n_main].astype(BF16)
        w_gate = jnp.pad(w_in[l, :, n_main:n_main + n_gate], ((0, 0), (0, LANES - n_gate))).astype(BF16)
        proj, gate_logits = _inproj(x2, norm_w[l][None, :], w_main, w_gate)
        gl = gate_logits[:, :n_gate].reshape(batch, seq, 2, n_dirs, n_heads)
        gl = jnp.transpose(gl[:, :, ::-1], (0, 4, 2, 3, 1)).reshape(batch, n_heads, 2 * n_dirs, seq)
        gl = jnp.pad(gl, ((0, 0), (0, 0), (0, 8 - 2 * n_dirs), (0, 0)))
        conv_w3 = jnp.pad(conv_w[l], ((0, 8 - CONV_K), (0, 0))).reshape(8, 3 * n_heads, HEAD_DIM)
        conv_w3 = jnp.transpose(conv_w3, (1, 0, 2))
        y_pool = _pool(proj, pool_w[l].astype(BF16), pool_scale[l].reshape(N_POOL_GROUPS, 1, pool_g),
                       batch=batch, seq=seq)
        y_dn = _deltanet(proj, gl, conv_w3, a_log[l], dt_bias[l], head_norm_w[l][None, :],
                         batch=batch, seq=seq, n_heads=n_heads, qkv_block0=qkv_block0, z_block0=z_block0)
        x2 = _outproj(y_pool, y_dn, x2, w_out[l, :pool_width].astype(BF16), w_out[l, pool_width:].astype(BF16),
                      final_norm_w[None, :], final=(l == depth - 1))
    return x2.reshape(batch, seq, d_model)
```

```python
import functools

import jax
import jax.numpy as jnp
from jax import lax
from jax.experimental import pallas as pl
from jax.experimental.pallas import tpu as pltpu

F32 = jnp.float32
BF16 = jnp.bfloat16

EPS = 1e-6
POOL_WINDOWS = (2, 4, 8, 16)
N_POOL_GROUPS = len(POOL_WINDOWS)
HEAD_DIM = 128
CONV_K = 5
CHUNK = 64
GROUP = 256
CHUNKS_PER_GROUP = GROUP // CHUNK
GROUP_UNROLL = 4
LANES = 128
MIB = 1024 * 1024


def _sigmoid(x):
    return 0.5 * jnp.tanh(0.5 * x) + 0.5


def _silu(x):
    return x * _sigmoid(x)


def _dot(a, b):
    return jnp.dot(a, b, preferred_element_type=F32)


def _inproj_body(x_ref, nw_ref, w_ref, wg_ref, o_ref, g_ref, h_scr, *, col_tile):
    x = x_ref[...]
    ms = jnp.mean(x * x, axis=-1, keepdims=True)
    h_scr[...] = (x * lax.rsqrt(ms + EPS) * nw_ref[...]).astype(BF16)
    g_ref[...] = _dot(h_scr[...], wg_ref[...])
    for n in range(w_ref.shape[1] // col_tile):
        cols = slice(n * col_tile, (n + 1) * col_tile)
        o_ref[:, cols] = _dot(h_scr[...], w_ref[:, cols]).astype(BF16)


def _inproj(x2, norm_w, w_main, w_gate, *, row_tile=512, col_tile=512):
    m, d = x2.shape
    n_main = w_main.shape[1]
    return pl.pallas_call(
        functools.partial(_inproj_body, col_tile=col_tile),
        grid=(m // row_tile,),
        in_specs=[
            pl.BlockSpec((row_tile, d), lambda i: (i, 0)),
            pl.BlockSpec((1, d), lambda i: (0, 0)),
            pl.BlockSpec((d, n_main), lambda i: (0, 0), pipeline_mode=pl.Buffered(1)),
            pl.BlockSpec((d, LANES), lambda i: (0, 0)),
        ],
        out_specs=[
            pl.BlockSpec((row_tile, n_main), lambda i: (i, 0)),
            pl.BlockSpec((row_tile, LANES), lambda i: (i, 0)),
        ],
        out_shape=[
            jax.ShapeDtypeStruct((m, n_main), BF16),
            jax.ShapeDtypeStruct((m, LANES), F32),
        ],
        scratch_shapes=[pltpu.VMEM((row_tile, d), BF16)],
        compiler_params=pltpu.CompilerParams(
            dimension_semantics=("arbitrary",), vmem_limit_bytes=48 * MIB),
        name="inproj",
    )(x2, norm_w, w_main, w_gate)


POOL_TILE = 256
POOL_HALO = 16


def _pool_body(u_ref, z_ref, pw_ref, ps_ref, o_ref):
    seq = u_ref.shape[0]
    g = pl.program_id(1)
    left = jnp.left_shift(jnp.int32(1), g)
    right = left - 1
    i = lax.broadcasted_iota(jnp.int32, (POOL_TILE, POOL_TILE), 0)
    j = lax.broadcasted_iota(jnp.int32, (POOL_TILE, POOL_TILE), 1)
    band = ((j >= i - left) & (j <= i + right)).astype(BF16)
    ih = lax.broadcasted_iota(jnp.int32, (POOL_TILE, POOL_HALO), 0)
    ph = lax.broadcasted_iota(jnp.int32, (POOL_TILE, POOL_HALO), 1)
    band_prev = (ph - POOL_HALO >= ih - left).astype(BF16)
    band_next = (ph + POOL_TILE <= ih + right).astype(BF16)
    pw = pw_ref[...]
    ps = ps_ref[...]
    tok0 = lax.broadcasted_iota(jnp.int32, (POOL_TILE, u_ref.shape[1]), 0)
    n_tiles = seq // POOL_TILE
    for t in range(n_tiles):
        t0 = t * POOL_TILE
        u = u_ref[t0:t0 + POOL_TILE, :]
        wsum = _dot(band, u)
        if t > 0:
            wsum += _dot(band_prev, u_ref[t0 - POOL_HALO:t0, :])
        if t < n_tiles - 1:
            wsum += _dot(band_next, u_ref[t0 + POOL_TILE:t0 + POOL_TILE + POOL_HALO, :])
        tok = tok0 + t0
        cnt = jnp.minimum(tok + right, seq - 1) - jnp.maximum(tok - left, 0) + 1
        m = wsum / cnt.astype(F32) - u.astype(F32)
        y = _dot(m.astype(BF16), pw) * ps
        o_ref[t0:t0 + POOL_TILE, :] = (y * _silu(z_ref[t0:t0 + POOL_TILE, :].astype(F32))).astype(BF16)


def _pool(proj, pool_w, pool_scale, *, batch, seq):
    pool_g = pool_w.shape[-1]
    return pl.pallas_call(
        _pool_body,
        grid=(batch, N_POOL_GROUPS),
        in_specs=[
            pl.BlockSpec((seq, pool_g), lambda b, g: (b, g)),
            pl.BlockSpec((seq, pool_g), lambda b, g: (b, N_POOL_GROUPS + g)),
            pl.BlockSpec((None, pool_g, pool_g), lambda b, g: (g, 0, 0)),
            pl.BlockSpec((None, 1, pool_g), lambda b, g: (g, 0, 0)),
        ],
        out_specs=pl.BlockSpec((seq, pool_g), lambda b, g: (b, g)),
        out_shape=jax.ShapeDtypeStruct((batch * seq, N_POOL_GROUPS * pool_g), BF16),
        compiler_params=pltpu.CompilerParams(
            dimension_semantics=("arbitrary", "arbitrary"), vmem_limit_bytes=40 * MIB),
        name="pool",
    )(proj, proj, pool_w, pool_scale)


CONV_TILE = 512
CONV_PAD = 16


def _seg_scan(x, pos, *, suffix):
    n = x.shape[1]
    s = 1
    while s < CHUNK:
        if suffix:
            x = x + jnp.where(pos < CHUNK - s, pltpu.roll(x, n - s, axis=1), 0.0)
        else:
            x = x + jnp.where(pos >= s, pltpu.roll(x, s, axis=1), 0.0)
        s *= 2
    return x


def _dn_body(alog_ref, dtb_ref, q_ref, k_ref, v_ref, z_ref, gl_ref, cwq_ref, cwk_ref, cwv_ref, hnw_ref,
             o_ref,
             xpad, qh, kh, vh, kt, gc_row, er_row, rep, aq_scr, b_scr, s_scr, o_acc):
    seq = q_ref.shape[0]
    n_chunks = seq // CHUNK
    h = pl.program_id(1)

    xpad[0:CONV_PAD, :] = jnp.zeros((CONV_PAD, HEAD_DIM), F32)
    xpad[CONV_PAD + seq:CONV_PAD + seq + CONV_PAD, :] = jnp.zeros((CONV_PAD, HEAD_DIM), F32)
    for which, (src, cw_ref) in enumerate(((q_ref, cwq_ref), (k_ref, cwk_ref), (v_ref, cwv_ref))):
        for t in range(seq // CONV_TILE):
            t0 = t * CONV_TILE
            xpad[CONV_PAD + t0:CONV_PAD + t0 + CONV_TILE, :] = src[t0:t0 + CONV_TILE, :].astype(F32)
        for t in range(seq // CONV_TILE):
            t0 = t * CONV_TILE
            acc = jnp.zeros((CONV_TILE, HEAD_DIM), F32)
            for tap in range(CONV_K):
                lo = CONV_PAD + t0 + tap - CONV_K // 2
                acc = acc + cw_ref[tap:tap + 1, :] * xpad[lo:lo + CONV_TILE, :]
            s = _silu(acc)
            rows = slice(t0, t0 + CONV_TILE)
            if which == 0:
                s = s * lax.rsqrt(jnp.sum(s * s, axis=-1, keepdims=True) + EPS) * (HEAD_DIM ** -0.5)
                qh[rows, :] = s.astype(BF16)
            elif which == 1:
                s = s * lax.rsqrt(jnp.sum(s * s, axis=-1, keepdims=True) + EPS)
                kh[rows, :] = s.astype(BF16)
                kt[:, rows] = s.T.astype(BF16)
            else:
                vh[rows, :] = s.astype(BF16)

    r = gl_ref[...]
    rowi = lax.broadcasted_iota(jnp.int32, r.shape, 0)
    pos = lax.broadcasted_iota(jnp.int32, r.shape, 1) & (CHUNK - 1)
    fwd_row = rowi == 0
    dtb = jnp.where(fwd_row, dtb_ref[0, h], dtb_ref[1, h])
    a_coef = jnp.exp(jnp.where(fwd_row, alog_ref[0, h], alog_ref[1, h]))
    g = jnp.where(rowi < 2, -a_coef * jax.nn.softplus(r + dtb), 0.0)
    pre = _seg_scan(g, pos, suffix=False)
    suf = _seg_scan(g, pos, suffix=True)
    gc = jnp.where(fwd_row, pre, suf)
    tot = pre + suf - g
    gc_row[...] = gc
    er_row[...] = jnp.exp(tot - gc)
    beta = _sigmoid(r)
    for row, src in enumerate((gc, gc, beta, beta)):
        for t in range(seq // LANES):
            tile = jnp.broadcast_to(src[row:row + 1, t * LANES:(t + 1) * LANES], (LANES, LANES))
            rep[row, t * LANES:(t + 1) * LANES, :] = tile.T

    ci = lax.broadcasted_iota(jnp.int32, (CHUNK, GROUP), 0)
    cl = lax.broadcasted_iota(jnp.int32, (CHUNK, GROUP), 1)
    cj = cl & (CHUNK - 1)
    lane_chunk = cl // CHUNK
    eye_c = (cj == ci).astype(F32)
    first_half = lax.broadcasted_iota(jnp.int32, (CHUNK, LANES), 1) < CHUNK
    bi = lax.broadcasted_iota(jnp.int32, (GROUP, GROUP), 0)
    bj = lax.broadcasted_iota(jnp.int32, (GROUP, GROUP), 1)
    bd_mask = ((bi // CHUNK) == (bj // CHUNK)).astype(F32).astype(BF16)

    def compact(full):
        out = full[(CHUNKS_PER_GROUP - 1) * CHUNK:, :]
        for c in range(CHUNKS_PER_GROUP - 2, -1, -1):
            out = jnp.where(lane_chunk == c, full[c * CHUNK:(c + 1) * CHUNK, :], out)
        return out

    def per_chunk_lanes(tok):
        blk = [tok[c * CHUNK:(c + 1) * CHUNK, :] for c in range(CHUNKS_PER_GROUP)]
        return jnp.concatenate([jnp.where(first_half, blk[0], blk[1]), jnp.where(first_half, blk[2], blk[3])], axis=1)

    def block_diag(cmp):
        cb = cmp.astype(BF16)
        return jnp.concatenate([cb] * CHUNKS_PER_GROUP, axis=0) * bd_mask

    def chunk_lanes(x, c):
        half = x[:, (c // 2) * LANES:(c // 2 + 1) * LANES]
        if c % 2:
            half = pltpu.roll(half, CHUNK, axis=1)
        return half[:, :CHUNK]

    o_acc[...] = jnp.zeros(o_acc.shape, F32)

    def group_step(t, carry):
        chains = []
        for k in range(GROUP_UNROLL):
            gi = t * GROUP_UNROLL + k
            r0 = pl.multiple_of(gi * GROUP, GROUP)
            rows = pl.ds(r0, GROUP)
            kt_g = kt[:, rows]
            kk = compact(_dot(kh[rows, :], kt_g))
            qk = compact(_dot(qh[rows, :], kt_g))
            for d in range(2):
                incl = (cj <= ci) if d == 0 else (cj >= ci)
                strict = (cj < ci) if d == 0 else (cj > ci)
                diff = per_chunk_lanes(rep[d, rows, :]) - gc_row[d:d + 1, rows]
                decay = jnp.where(incl, jnp.exp(jnp.where(incl, diff, 0.0)), 0.0)
                mk = jnp.where(strict, -(per_chunk_lanes(rep[2 + d, rows, :]) * kk * decay), 0.0)
                a = jnp.where(incl, qk * decay, 0.0)
                chains.append(dict(gi=gi, r0=r0, rows=rows, d=d, mk=mk, tinv=eye_c + mk, a=a))
        for ch in chains:
            ch["mk"] = _dot(ch["mk"].astype(BF16), block_diag(ch["mk"]))
        p = 2
        while p < CHUNK // 2:
            for ch in chains:
                x2 = _dot(jnp.concatenate([ch["tinv"], ch["mk"]], axis=0).astype(BF16), block_diag(ch["mk"]))
                ch["tinv"] = ch["tinv"] + x2[:CHUNK]
                ch["mk"] = x2[CHUNK:]
            p *= 2
        for ch in chains:
            ch["tinv"] = ch["tinv"] + _dot(ch["tinv"].astype(BF16), block_diag(ch["mk"]))
        for ch in chains:
            d, rows = ch["d"], ch["rows"]
            gc_tok = rep[d, rows, :]
            beta_tok = rep[2 + d, rows, :]
            egc = jnp.exp(gc_tok)
            kf = kh[rows, :].astype(F32)
            vf = vh[rows, :].astype(F32)
            rhs = jnp.concatenate([kf * (beta_tok * egc), vf * beta_tok], axis=1).astype(BF16)
            ch["wu"] = _dot(block_diag(ch["tinv"]), rhs).astype(BF16)
        for ch in chains:
            d, rows = ch["d"], ch["rows"]
            q_til = qh[rows, :].astype(F32) * jnp.exp(rep[d, rows, :])
            lhs_all = jnp.concatenate([kt[:, rows].astype(F32) * er_row[d:d + 1, rows], ch["a"]], axis=0)
            for c in range(CHUNKS_PER_GROUP):
                crow = slice(c * CHUNK, (c + 1) * CHUNK)
                res = _dot(chunk_lanes(lhs_all, c).astype(BF16), ch["wu"][crow, :])
                cidx = ch["gi"] * CHUNKS_PER_GROUP + c
                aq_scr[d, cidx, 0:HEAD_DIM, :] = res[:HEAD_DIM, :HEAD_DIM].astype(BF16)
                aq_scr[d, cidx, HEAD_DIM:, :] = (q_til[crow, :] - res[HEAD_DIM:, :HEAD_DIM]).astype(BF16)
                b_scr[d, cidx] = res[:HEAD_DIM, HEAD_DIM:]
                o_acc[pl.ds(ch["r0"] + c * CHUNK, CHUNK), :] += res[HEAD_DIM:, HEAD_DIM:]
        return carry

    lax.fori_loop(0, seq // GROUP // GROUP_UNROLL, group_step, 0)

    s_scr[...] = jnp.zeros(s_scr.shape, F32)

    def chunk_step(t, carry):
        for d in range(2):
            c = t if d == 0 else n_chunks - 1 - t
            r0 = pl.multiple_of(c * CHUNK, CHUNK)
            state = s_scr[d]
            res = _dot(aq_scr[d, c], state.astype(BF16))
            last = r0 + (CHUNK - 1 if d == 0 else 0)
            dec = jnp.exp(rep[d, pl.ds(last, 1), :])
            s_scr[d] = state * dec - res[:HEAD_DIM] + b_scr[d, c]
            o_acc[pl.ds(r0, CHUNK), :] += res[HEAD_DIM:]
        return carry

    lax.fori_loop(0, n_chunks, chunk_step, 0)

    hnw = hnw_ref[...]
    for t in range(seq // CONV_TILE):
        rows = slice(t * CONV_TILE, (t + 1) * CONV_TILE)
        o = o_acc[rows, :]
        y = o * lax.rsqrt(jnp.mean(o * o, axis=-1, keepdims=True) + EPS) * hnw
        o_ref[rows, :] = (y * _silu(z_ref[rows, :].astype(F32))).astype(BF16)


def _deltanet(proj, gate_rows, conv_w3, a_log, dt_bias, head_norm_w, *, batch, seq, n_heads, qkv_block0, z_block0):
    n_chunks = seq // CHUNK
    smem = pl.BlockSpec(memory_space=pltpu.SMEM)
    head_cols = lambda off: pl.BlockSpec((seq, HEAD_DIM), lambda b, h: (b, off + h))
    conv_cols = lambda off: pl.BlockSpec((None, 8, HEAD_DIM), lambda b, h: (off + h, 0, 0))
    return pl.pallas_call(
        _dn_body,
        grid=(batch, n_heads),
        in_specs=[
            smem, smem,
            head_cols(qkv_block0), head_cols(qkv_block0 + n_heads), head_cols(qkv_block0 + 2 * n_heads),
            head_cols(z_block0),
            pl.BlockSpec((None, None, 8, seq), lambda b, h: (b, h, 0, 0)),
            conv_cols(0), conv_cols(n_heads), conv_cols(2 * n_heads),
            pl.BlockSpec((1, HEAD_DIM), lambda b, h: (0, 0)),
        ],
        out_specs=pl.BlockSpec((seq, HEAD_DIM), lambda b, h: (b, h)),
        out_shape=jax.ShapeDtypeStruct((batch * seq, n_heads * HEAD_DIM), BF16),
        scratch_shapes=[
            pltpu.VMEM((seq + 2 * CONV_PAD, HEAD_DIM), F32),
            pltpu.VMEM((seq, HEAD_DIM), BF16),
            pltpu.VMEM((seq, HEAD_DIM), BF16),
            pltpu.VMEM((seq, HEAD_DIM), BF16),
            pltpu.VMEM((HEAD_DIM, seq), BF16),
            pltpu.VMEM((8, seq), F32),
            pltpu.VMEM((8, seq), F32),
            pltpu.VMEM((4, seq, LANES), F32),
            pltpu.VMEM((2, n_chunks, HEAD_DIM + CHUNK, HEAD_DIM), BF16),
            pltpu.VMEM((2, n_chunks, HEAD_DIM, HEAD_DIM), F32),
            pltpu.VMEM((2, HEAD_DIM, HEAD_DIM), F32),
            pltpu.VMEM((seq, HEAD_DIM), F32),
        ],
        compiler_params=pltpu.CompilerParams(
            dimension_semantics=("arbitrary", "arbitrary"), vmem_limit_bytes=56 * MIB),
        name="deltanet",
    )(a_log, dt_bias, proj, proj, proj, proj, gate_rows, conv_w3, conv_w3, conv_w3, head_norm_w)


def _outproj_body(yp_ref, yd_ref, x_ref, w1_ref, w2_ref, fnw_ref, o_ref, *, final):
    acc = x_ref[...] + _dot(yp_ref[...], w1_ref[...]) + _dot(yd_ref[...], w2_ref[...])
    if final:
        acc = acc * lax.rsqrt(jnp.mean(acc * acc, axis=-1, keepdims=True) + EPS) * fnw_ref[...]
    o_ref[...] = acc


def _outproj(y_pool, y_dn, x2, w1, w2, final_norm_w, *, final, row_tile=512):
    m, d = x2.shape
    k1, k2 = w1.shape[0], w2.shape[0]
    return pl.pallas_call(
        functools.partial(_outproj_body, final=final),
        grid=(m // row_tile,),
        in_specs=[
            pl.BlockSpec((row_tile, k1), lambda i: (i, 0)),
            pl.BlockSpec((row_tile, k2), lambda i: (i, 0)),
            pl.BlockSpec((row_tile, d), lambda i: (i, 0)),
            pl.BlockSpec((k1, d), lambda i: (0, 0)),
            pl.BlockSpec((k2, d), lambda i: (0, 0)),
            pl.BlockSpec((1, d), lambda i: (0, 0)),
        ],
        out_specs=pl.BlockSpec((row_tile, d), lambda i: (i, 0)),
        out_shape=jax.ShapeDtypeStruct((m, d), F32),
        compiler_params=pltpu.CompilerParams(
            dimension_semantics=("arbitrary",), vmem_limit_bytes=40 * MIB),
        name="outproj",
    )(y_pool, y_dn, x2, w1, w2, final_norm_w)


def kernel(x, norm_w, w_in, pool_w, pool_scale, conv_w, a_log, dt_bias, head_norm_w, w_out, final_norm_w):
    batch, seq, d_model = x.shape
    depth = norm_w.shape[0]
    pool_g = pool_w.shape[-1]
    pool_width = N_POOL_GROUPS * pool_g
    n_dirs, n_heads = a_log.shape[1], a_log.shape[2]
    dn_width = n_heads * HEAD_DIM
    n_main = 2 * pool_width + 4 * dn_width
    n_gate = 2 * n_dirs * n_heads
    qkv_block0 = 2 * pool_width // HEAD_DIM
    z_block0 = qkv_block0 + 3 * n_heads

    x2 = x.reshape(batch * seq, d_model)
    for l in range(depth):
        w_main = w_in[l, :, :n_main].astype(BF16)
        w_gate = jnp.pad(w_in[l, :, n_main:n_main + n_gate], ((0, 0), (0, LANES - n_gate))).astype(BF16)
        proj, gate_logits = _inproj(x2, norm_w[l][None, :], w_main, w_gate)
        gl = gate_logits[:, :n_gate].reshape(batch, seq, 2, n_dirs, n_heads)
        gl = jnp.transpose(gl[:, :, ::-1], (0, 4, 2, 3, 1)).reshape(batch, n_heads, 2 * n_dirs, seq)
        gl = jnp.pad(gl, ((0, 0), (0, 0), (0, 8 - 2 * n_dirs), (0, 0)))
        conv_w3 = jnp.pad(conv_w[l], ((0, 8 - CONV_K), (0, 0))).reshape(8, 3 * n_heads, HEAD_DIM)
        conv_w3 = jnp.transpose(conv_w3, (1, 0, 2))
        y_pool = _pool(proj, pool_w[l].astype(BF16), pool_scale[l].reshape(N_POOL_GROUPS, 1, pool_g),
                       batch=batch, seq=seq)
        y_dn = _deltanet(proj, gl, conv_w3, a_log[l], dt_bias[l], head_norm_w[l][None, :],
                         batch=batch, seq=seq, n_heads=n_heads, qkv_block0=qkv_block0, z_block0=z_block0)
        x2 = _outproj(y_pool, y_dn, x2, w_out[l, :pool_width].astype(BF16), w_out[l, pool_width:].astype(BF16),
                      final_norm_w[None, :], final=(l == depth - 1))
    return x2.reshape(batch, seq, d_model)
```

```python
import functools

import jax
import jax.numpy as jnp
from jax import lax
from jax.experimental import pallas as pl
from jax.experimental.pallas import tpu as pltpu

F32 = jnp.float32
BF16 = jnp.bfloat16

EPS = 1e-6
POOL_WINDOWS = (2, 4, 8, 16)
N_POOL_GROUPS = len(POOL_WINDOWS)
HEAD_DIM = 128
CONV_K = 5
CHUNK = 64
GROUP = 256
CHUNKS_PER_GROUP = GROUP // CHUNK
GROUP_UNROLL = 4
REC_EVERY = 6
LANES = 128
MIB = 1024 * 1024


def _sigmoid(x):
    return 0.5 * jnp.tanh(0.5 * x) + 0.5


def _silu(x):
    half = 0.5 * x
    return half + half * jnp.tanh(half)


def _dot(a, b):
    return jnp.dot(a, b, preferred_element_type=F32)


def _inproj_body(x_ref, nw_ref, w_ref, wg_ref, o_ref, g_ref, h_scr, *, col_tile):
    x = x_ref[...]
    ms = jnp.mean(x * x, axis=-1, keepdims=True)
    h_scr[...] = (x * lax.rsqrt(ms + EPS) * nw_ref[...]).astype(BF16)
    g_ref[...] = _dot(h_scr[...], wg_ref[...])
    for n in range(w_ref.shape[1] // col_tile):
        cols = slice(n * col_tile, (n + 1) * col_tile)
        o_ref[:, cols] = _dot(h_scr[...], w_ref[:, cols]).astype(BF16)


def _inproj(x2, norm_w, w_main, w_gate, *, row_tile=512, col_tile=512):
    m, d = x2.shape
    n_main = w_main.shape[1]
    return pl.pallas_call(
        functools.partial(_inproj_body, col_tile=col_tile),
        grid=(m // row_tile,),
        in_specs=[
            pl.BlockSpec((row_tile, d), lambda i: (i, 0)),
            pl.BlockSpec((1, d), lambda i: (0, 0)),
            pl.BlockSpec((d, n_main), lambda i: (0, 0), pipeline_mode=pl.Buffered(1)),
            pl.BlockSpec((d, LANES), lambda i: (0, 0)),
        ],
        out_specs=[
            pl.BlockSpec((row_tile, n_main), lambda i: (i, 0)),
            pl.BlockSpec((row_tile, LANES), lambda i: (i, 0)),
        ],
        out_shape=[
            jax.ShapeDtypeStruct((m, n_main), BF16),
            jax.ShapeDtypeStruct((m, LANES), F32),
        ],
        scratch_shapes=[pltpu.VMEM((row_tile, d), BF16)],
        compiler_params=pltpu.CompilerParams(
            dimension_semantics=("arbitrary",), vmem_limit_bytes=48 * MIB),
        name="inproj",
    )(x2, norm_w, w_main, w_gate)


POOL_TILE = 256
POOL_HALO = 16


def _pool_body(u_ref, z_ref, pw_ref, ps_ref, o_ref):
    seq = u_ref.shape[0]
    g = pl.program_id(1)
    left = jnp.left_shift(jnp.int32(1), g)
    right = left - 1
    i = lax.broadcasted_iota(jnp.int32, (POOL_TILE, POOL_TILE), 0)
    j = lax.broadcasted_iota(jnp.int32, (POOL_TILE, POOL_TILE), 1)
    band = ((j >= i - left) & (j <= i + right)).astype(BF16)
    ih = lax.broadcasted_iota(jnp.int32, (POOL_TILE, 2 * POOL_HALO), 0)
    ph = lax.broadcasted_iota(jnp.int32, (POOL_TILE, 2 * POOL_HALO), 1)
    in_prev = (ph < POOL_HALO) & (ph - POOL_HALO >= ih - left)
    in_next = (ph >= POOL_HALO) & (ph - POOL_HALO + POOL_TILE <= ih + right)
    halo_band = (in_prev | in_next).astype(BF16)
    pw = pw_ref[...]
    ps = ps_ref[...]
    n_tiles = seq // POOL_TILE
    tok0 = lax.broadcasted_iota(jnp.int32, (POOL_TILE, u_ref.shape[1]), 0)

    def inv_count(t0):
        tok = tok0 + t0
        cnt = jnp.minimum(tok + right, seq - 1) - jnp.maximum(tok - left, 0) + 1
        return 1.0 / cnt.astype(F32)

    inv_first, inv_mid, inv_last = inv_count(0), inv_count(POOL_TILE), inv_count(seq - POOL_TILE)
    zero_halo = jnp.zeros((POOL_HALO, u_ref.shape[1]), BF16)

    def window_sums(t):
        t0 = t * POOL_TILE
        prev = u_ref[t0 - POOL_HALO:t0, :] if t > 0 else zero_halo
        nxt = u_ref[t0 + POOL_TILE:t0 + POOL_TILE + POOL_HALO, :] if t < n_tiles - 1 else zero_halo
        return _dot(band, u_ref[t0:t0 + POOL_TILE, :]) + _dot(halo_band, jnp.concatenate([prev, nxt], axis=0))

    wsum = window_sums(0)
    for t in range(n_tiles):
        t0 = t * POOL_TILE
        wsum_next = window_sums(t + 1) if t < n_tiles - 1 else None
        inv = inv_first if t == 0 else (inv_last if t == n_tiles - 1 else inv_mid)
        m = wsum * inv - u_ref[t0:t0 + POOL_TILE, :].astype(F32)
        y = _dot(m.astype(BF16), pw) * ps
        o_ref[t0:t0 + POOL_TILE, :] = (y * _silu(z_ref[t0:t0 + POOL_TILE, :].astype(F32))).astype(BF16)
        wsum = wsum_next


def _pool(proj, pool_w, pool_scale, *, batch, seq):
    pool_g = pool_w.shape[-1]
    return pl.pallas_call(
        _pool_body,
        grid=(batch, N_POOL_GROUPS),
        in_specs=[
            pl.BlockSpec((seq, pool_g), lambda b, g: (b, g)),
            pl.BlockSpec((seq, pool_g), lambda b, g: (b, N_POOL_GROUPS + g)),
            pl.BlockSpec((None, pool_g, pool_g), lambda b, g: (g, 0, 0)),
            pl.BlockSpec((None, 1, pool_g), lambda b, g: (g, 0, 0)),
        ],
        out_specs=pl.BlockSpec((seq, pool_g), lambda b, g: (b, g)),
        out_shape=jax.ShapeDtypeStruct((batch * seq, N_POOL_GROUPS * pool_g), BF16),
        compiler_params=pltpu.CompilerParams(
            dimension_semantics=("arbitrary", "arbitrary"), vmem_limit_bytes=40 * MIB),
        name="pool",
    )(proj, proj, pool_w, pool_scale)


CONV_TILE = 512
CONV_PAD = 16


def _seg_scan(x, pos, *, suffix):
    n = x.shape[1]
    s = 1
    while s < CHUNK:
        if suffix:
            x = x + jnp.where(pos < CHUNK - s, pltpu.roll(x, n - s, axis=1), 0.0)
        else:
            x = x + jnp.where(pos >= s, pltpu.roll(x, s, axis=1), 0.0)
        s *= 2
    return x


def _dn_body(alog_ref, dtb_ref, q_ref, k_ref, v_ref, z_ref, gl_ref, cwq_ref, cwk_ref, cwv_ref, hnw_ref,
             o_ref,
             xpad, qh, kh, vh, kt, gc_row, er_row, beta_row, rep, aq_scr, b_scr, s_scr, o_acc):
    seq = q_ref.shape[0]
    n_chunks = seq // CHUNK
    h = pl.program_id(1)

    xpad[0:CONV_PAD, :] = jnp.zeros((CONV_PAD, HEAD_DIM), F32)
    xpad[CONV_PAD + seq:CONV_PAD + seq + CONV_PAD, :] = jnp.zeros((CONV_PAD, HEAD_DIM), F32)
    for which, (src, cw_ref) in enumerate(((q_ref, cwq_ref), (k_ref, cwk_ref), (v_ref, cwv_ref))):
        for t in range(seq // CONV_TILE):
            t0 = t * CONV_TILE
            xpad[CONV_PAD + t0:CONV_PAD + t0 + CONV_TILE, :] = src[t0:t0 + CONV_TILE, :].astype(F32)
        for t in range(seq // CONV_TILE):
            t0 = t * CONV_TILE
            acc = jnp.zeros((CONV_TILE, HEAD_DIM), F32)
            for tap in range(CONV_K):
                lo = CONV_PAD + t0 + tap - CONV_K // 2
                acc = acc + cw_ref[tap:tap + 1, :] * xpad[lo:lo + CONV_TILE, :]
            s = _silu(acc)
            rows = slice(t0, t0 + CONV_TILE)
            if which == 0:
                s = s * lax.rsqrt(jnp.sum(s * s, axis=-1, keepdims=True) + EPS) * (HEAD_DIM ** -0.5)
                qh[rows, :] = s.astype(BF16)
            elif which == 1:
                s = s * lax.rsqrt(jnp.sum(s * s, axis=-1, keepdims=True) + EPS)
                kh[rows, :] = s.astype(BF16)
                kt[:, rows] = s.T.astype(BF16)
            else:
                vh[rows, :] = s.astype(BF16)

    r = gl_ref[...]
    rowi = lax.broadcasted_iota(jnp.int32, r.shape, 0)
    pos = lax.broadcasted_iota(jnp.int32, r.shape, 1) & (CHUNK - 1)
    fwd_row = rowi == 0
    dtb = jnp.where(fwd_row, dtb_ref[0, h], dtb_ref[1, h])
    a_coef = jnp.exp(jnp.where(fwd_row, alog_ref[0, h], alog_ref[1, h]))
    g = jnp.where(rowi < 2, -a_coef * jax.nn.softplus(r + dtb), 0.0)
    pre = _seg_scan(g, pos, suffix=False)
    suf = _seg_scan(g, pos, suffix=True)
    gc = jnp.where(fwd_row, pre, suf)
    tot = pre + suf - g
    gc_row[...] = gc
    er_row[...] = jnp.exp(tot - gc)
    beta_row[...] = _sigmoid(r)
    for d in range(2):
        for t in range(seq // LANES):
            tile = jnp.broadcast_to(gc[d:d + 1, t * LANES:(t + 1) * LANES], (LANES, LANES))
            rep[d, t * LANES:(t + 1) * LANES, :] = tile.T

    ci = lax.broadcasted_iota(jnp.int32, (CHUNK, GROUP), 0)
    cl = lax.broadcasted_iota(jnp.int32, (CHUNK, GROUP), 1)
    cj = cl & (CHUNK - 1)
    lane_chunk = cl // CHUNK
    eye_c = (cj == ci).astype(F32)
    first_half = lax.broadcasted_iota(jnp.int32, (CHUNK, LANES), 1) < CHUNK
    bi = lax.broadcasted_iota(jnp.int32, (GROUP, GROUP), 0)
    bj = lax.broadcasted_iota(jnp.int32, (GROUP, GROUP), 1)
    bd_mask = ((bi // CHUNK) == (bj // CHUNK)).astype(F32).astype(BF16)

    def compact(full):
        out = full[(CHUNKS_PER_GROUP - 1) * CHUNK:, :]
        for c in range(CHUNKS_PER_GROUP - 2, -1, -1):
            out = jnp.where(lane_chunk == c, full[c * CHUNK:(c + 1) * CHUNK, :], out)
        return out

    def per_chunk_lanes(tok):
        blk = [tok[c * CHUNK:(c + 1) * CHUNK, :] for c in range(CHUNKS_PER_GROUP)]
        return jnp.concatenate([jnp.where(first_half, blk[0], blk[1]), jnp.where(first_half, blk[2], blk[3])], axis=1)

    def block_diag(cmp):
        cb = cmp.astype(BF16)
        return jnp.concatenate([cb] * CHUNKS_PER_GROUP, axis=0) * bd_mask

    def chunk_lanes(x, c):
        half = x[:, (c // 2) * LANES:(c // 2 + 1) * LANES]
        if c % 2:
            half = pltpu.roll(half, CHUNK, axis=1)
        return half[:, :CHUNK]

    o_acc[...] = jnp.zeros(o_acc.shape, F32)

    s_scr[...] = jnp.zeros(s_scr.shape, F32)
    n_groups = seq // GROUP
    n_trips = n_groups // GROUP_UNROLL
    chunks_per_trip = GROUP_UNROLL * CHUNKS_PER_GROUP

    def recurrence_step(states, k, trip):
        for d in range(2):
            lin = trip * chunks_per_trip + k
            c = lin if d == 0 else n_chunks - 1 - lin
            r0 = pl.multiple_of(c * CHUNK, CHUNK)
            res = _dot(aq_scr[d, c], states[d].astype(BF16))
            last = r0 + (CHUNK - 1 if d == 0 else 0)
            dec = jnp.exp(rep[d, pl.ds(last, 1), :])
            states[d] = states[d] * dec - res[:HEAD_DIM] + b_scr[d, c]
            o_acc[pl.ds(r0, CHUNK), :] += res[HEAD_DIM:]

    def precompute(trip, tick):
        chains = []
        for d in range(2):
            incl = (cj <= ci) if d == 0 else (cj >= ci)
            strict = (cj < ci) if d == 0 else (cj > ci)
            for k in range(GROUP_UNROLL):
                lin = trip * GROUP_UNROLL + k
                gi = lin if d == 0 else n_groups - 1 - lin
                r0 = pl.multiple_of(gi * GROUP, GROUP)
                rows = pl.ds(r0, GROUP)
                kt_g = kt[:, rows]
                kk = compact(_dot(kh[rows, :], kt_g))
                tick()
                qk = compact(_dot(qh[rows, :], kt_g))
                tick()
                diff = per_chunk_lanes(rep[d, rows, :]) - gc_row[d:d + 1, rows]
                decay = jnp.where(incl, jnp.exp(jnp.where(incl, diff, 0.0)), 0.0)
                decay_beta = decay * beta_row[2 + d:3 + d, rows]
                mk = jnp.where(strict, -(kk * decay_beta), 0.0)
                a = jnp.where(incl, qk * decay_beta, 0.0)
                chains.append(dict(gi=gi, r0=r0, rows=rows, d=d, mk=mk, tinv=eye_c + mk, a=a))
        for ch in chains:
            ch["mk"] = _dot(ch["mk"].astype(BF16), block_diag(ch["mk"]))
            tick()
        p = 2
        while p < CHUNK // 2:
            for ch in chains:
                x2 = _dot(jnp.concatenate([ch["tinv"], ch["mk"]], axis=0).astype(BF16), block_diag(ch["mk"]))
                tick()
                ch["tinv"] = ch["tinv"] + x2[:CHUNK]
                ch["mk"] = x2[CHUNK:]
            p *= 2
        for ch in chains:
            ch["tinv"] = ch["tinv"] + _dot(ch["tinv"].astype(BF16), block_diag(ch["mk"]))
            tick()
        for ch in chains:
            d, rows = ch["d"], ch["rows"]
            ke = (kh[rows, :].astype(F32) * jnp.exp(rep[d, rows, :])).astype(BF16)
            rhs = jnp.concatenate([ke, vh[rows, :]], axis=1)
            ch["wu"] = _dot(block_diag(ch["tinv"]), rhs).astype(BF16)
            tick()
        for ch in chains:
            d, rows = ch["d"], ch["rows"]
            q_til = qh[rows, :].astype(F32) * jnp.exp(rep[d, rows, :])
            kt_til = kt[:, rows].astype(F32) * (er_row[d:d + 1, rows] * beta_row[2 + d:3 + d, rows])
            lhs_all = jnp.concatenate([kt_til, ch["a"]], axis=0)
            for c in range(CHUNKS_PER_GROUP):
                crow = slice(c * CHUNK, (c + 1) * CHUNK)
                res = _dot(chunk_lanes(lhs_all, c).astype(BF16), ch["wu"][crow, :])
                tick()
                cidx = ch["gi"] * CHUNKS_PER_GROUP + c
                aq_scr[d, cidx, 0:HEAD_DIM, :] = res[:HEAD_DIM, :HEAD_DIM].astype(BF16)
                aq_scr[d, cidx, HEAD_DIM:, :] = (q_til[crow, :] - res[HEAD_DIM:, :HEAD_DIM]).astype(BF16)
                b_scr[d, cidx] = res[:HEAD_DIM, HEAD_DIM:]
                o_acc[pl.ds(ch["r0"] + c * CHUNK, CHUNK), :] += res[HEAD_DIM:, HEAD_DIM:]

    precompute(0, lambda: None)

    def pipelined_trip(trip, carry):
        states = [s_scr[0], s_scr[1]]
        progress = dict(dots=0, steps=0)

        def tick():
            progress["dots"] += 1
            if progress["dots"] % REC_EVERY == 0 and progress["steps"] < chunks_per_trip:
                recurrence_step(states, progress["steps"], trip - 1)
                progress["steps"] += 1

        precompute(trip, tick)
        while progress["steps"] < chunks_per_trip:
            recurrence_step(states, progress["steps"], trip - 1)
            progress["steps"] += 1
        s_scr[0] = states[0]
        s_scr[1] = states[1]
        return carry

    lax.fori_loop(1, n_trips, pipelined_trip, 0)

    def drain_step(k, carry):
        states = [s_scr[0], s_scr[1]]
        recurrence_step(states, k, n_trips - 1)
        s_scr[0] = states[0]
        s_scr[1] = states[1]
        return carry

    lax.fori_loop(0, chunks_per_trip, drain_step, 0)

    hnw = hnw_ref[...]
    for t in range(seq // CONV_TILE):
        rows = slice(t * CONV_TILE, (t + 1) * CONV_TILE)
        o = o_acc[rows, :]
        y = o * lax.rsqrt(jnp.mean(o * o, axis=-1, keepdims=True) + EPS) * hnw
        o_ref[rows, :] = (y * _silu(z_ref[rows, :].astype(F32))).astype(BF16)


def _deltanet(proj, gate_rows, conv_w3, a_log, dt_bias, head_norm_w, *, batch, seq, n_heads, qkv_block0, z_block0):
    n_chunks = seq // CHUNK
    smem = pl.BlockSpec(memory_space=pltpu.SMEM)
    head_cols = lambda off: pl.BlockSpec((seq, HEAD_DIM), lambda b, h: (b, off + h))
    conv_cols = lambda off: pl.BlockSpec((None, 8, HEAD_DIM), lambda b, h: (off + h, 0, 0))
    return pl.pallas_call(
        _dn_body,
        grid=(batch, n_heads),
        in_specs=[
            smem, smem,
            head_cols(qkv_block0), head_cols(qkv_block0 + n_heads), head_cols(qkv_block0 + 2 * n_heads),
            head_cols(z_block0),
            pl.BlockSpec((None, None, 8, seq), lambda b, h: (b, h, 0, 0)),
            conv_cols(0), conv_cols(n_heads), conv_cols(2 * n_heads),
            pl.BlockSpec((1, HEAD_DIM), lambda b, h: (0, 0)),
        ],
        out_specs=pl.BlockSpec((seq, HEAD_DIM), lambda b, h: (b, h)),
        out_shape=jax.ShapeDtypeStruct((batch * seq, n_heads * HEAD_DIM), BF16),
        scratch_shapes=[
            pltpu.VMEM((seq + 2 * CONV_PAD, HEAD_DIM), F32),
            pltpu.VMEM((seq, HEAD_DIM), BF16),
            pltpu.VMEM((seq, HEAD_DIM), BF16),
            pltpu.VMEM((seq, HEAD_DIM), BF16),
            pltpu.VMEM((HEAD_DIM, seq), BF16),
            pltpu.VMEM((8, seq), F32),
            pltpu.VMEM((8, seq), F32),
            pltpu.VMEM((8, seq), F32),
            pltpu.VMEM((2, seq, LANES), F32),
            pltpu.VMEM((2, n_chunks, HEAD_DIM + CHUNK, HEAD_DIM), BF16),
            pltpu.VMEM((2, n_chunks, HEAD_DIM, HEAD_DIM), F32),
            pltpu.VMEM((2, HEAD_DIM, HEAD_DIM), F32),
            pltpu.VMEM((seq, HEAD_DIM), F32),
        ],
        compiler_params=pltpu.CompilerParams(
            dimension_semantics=("arbitrary", "arbitrary"), vmem_limit_bytes=56 * MIB),
        name="deltanet",
    )(a_log, dt_bias, proj, proj, proj, proj, gate_rows, conv_w3, conv_w3, conv_w3, head_norm_w)


def _outproj_body(yp_ref, yd_ref, x_ref, w1_ref, w2_ref, fnw_ref, o_ref, *, final):
    acc = x_ref[...] + _dot(yp_ref[...], w1_ref[...]) + _dot(yd_ref[...], w2_ref[...])
    if final:
        acc = acc * lax.rsqrt(jnp.mean(acc * acc, axis=-1, keepdims=True) + EPS) * fnw_ref[...]
    o_ref[...] = acc


def _outproj(y_pool, y_dn, x2, w1, w2, final_norm_w, *, final, row_tile=512):
    m, d = x2.shape
    k1, k2 = w1.shape[0], w2.shape[0]
    return pl.pallas_call(
        functools.partial(_outproj_body, final=final),
        grid=(m // row_tile,),
        in_specs=[
            pl.BlockSpec((row_tile, k1), lambda i: (i, 0)),
            pl.BlockSpec((row_tile, k2), lambda i: (i, 0)),
            pl.BlockSpec((row_tile, d), lambda i: (i, 0)),
            pl.BlockSpec((k1, d), lambda i: (0, 0)),
            pl.BlockSpec((k2, d), lambda i: (0, 0)),
            pl.BlockSpec((1, d), lambda i: (0, 0)),
        ],
        out_specs=pl.BlockSpec((row_tile, d), lambda i: (i, 0)),
        out_shape=jax.ShapeDtypeStruct((m, d), F32),
        compiler_params=pltpu.CompilerParams(
            dimension_semantics=("arbitrary",), vmem_limit_bytes=40 * MIB),
        name="outproj",
    )(y_pool, y_dn, x2, w1, w2, final_norm_w)


def kernel(x, norm_w, w_in, pool_w, pool_scale, conv_w, a_log, dt_bias, head_norm_w, w_out, final_norm_w):
    batch, seq, d_model = x.shape
    depth = norm_w.shape[0]
    pool_g = pool_w.shape[-1]
    pool_width = N_POOL_GROUPS * pool_g
    n_dirs, n_heads = a_log.shape[1], a_log.shape[2]
    dn_width = n_heads * HEAD_DIM
    n_main = 2 * pool_width + 4 * dn_width
    n_gate = 2 * n_dirs * n_heads
    qkv_block0 = 2 * pool_width // HEAD_DIM
    z_block0 = qkv_block0 + 3 * n_heads

    x2 = x.reshape(batch * seq, d_model)
    for l in range(depth):
        w_main = w_in[l, :, :n_main].astype(BF16)
        w_gate = jnp.pad(w_in[l, :, n_main:n_main + n_gate], ((0, 0), (0, LANES - n_gate))).astype(BF16)
        proj, gate_logits = _inproj(x2, norm_w[l][None, :], w_main, w_gate)
        gl = gate_logits[:, :n_gate].reshape(batch, seq, 2, n_dirs, n_heads)
        gl = jnp.transpose(gl[:, :, ::-1], (0, 4, 2, 3, 1)).reshape(batch, n_heads, 2 * n_dirs, seq)
        gl = jnp.pad(gl, ((0, 0), (0, 0), (0, 8 - 2 * n_dirs), (0, 0)))
        conv_w3 = jnp.pad(conv_w[l], ((0, 8 - CONV_K), (0, 0))).reshape(8, 3 * n_heads, HEAD_DIM)
        conv_w3 = jnp.transpose(conv_w3, (1, 0, 2))
        y_pool = _pool(proj, pool_w[l].astype(BF16), pool_scale[l].reshape(N_POOL_GROUPS, 1, pool_g),
                       batch=batch, seq=seq)
        y_dn = _deltanet(proj, gl, conv_w3, a_log[l], dt_bias[l], head_norm_w[l][None, :],
                         batch=batch, seq=seq, n_heads=n_heads, qkv_block0=qkv_block0, z_block0=z_block0)
        x2 = _outproj(y_pool, y_dn, x2, w_out[l, :pool_width].astype(BF16), w_out[l, pool_width:].astype(BF16),
                      final_norm_w[None, :], final=(l == depth - 1))
    return x2.reshape(batch, seq, d_model)
```

```python
import functools

import jax
import jax.numpy as jnp
from jax import lax
from jax.experimental import pallas as pl
from jax.experimental.pallas import tpu as pltpu

F32 = jnp.float32
BF16 = jnp.bfloat16

EPS = 1e-6
POOL_WINDOWS = (2, 4, 8, 16)
N_POOL_GROUPS = len(POOL_WINDOWS)
HEAD_DIM = 128
CONV_K = 5
CHUNK = 64
GROUP = 256
CHUNKS_PER_GROUP = GROUP // CHUNK
GROUP_UNROLL = 4
REC_EVERY = 6
PREP_EVERY = 10
GATE_ROWS = 8
LANES = 128
MIB = 1024 * 1024


def _sigmoid(x):
    return 0.5 * jnp.tanh(0.5 * x) + 0.5


def _silu(x):
    half = 0.5 * x
    return half + half * jnp.tanh(half)


def _dot(a, b):
    return jnp.dot(a, b, preferred_element_type=F32)


def _inproj_body(x_ref, nw_ref, w_ref, wg_ref, o_ref, g_ref, h_scr, *, col_tile):
    x = x_ref[...]
    ms = jnp.mean(x * x, axis=-1, keepdims=True)
    h_scr[...] = (x * lax.rsqrt(ms + EPS) * nw_ref[...]).astype(BF16)
    g_ref[...] = lax.dot_general(wg_ref[...], h_scr[...], (((1,), (1,)), ((), ())), preferred_element_type=F32)
    for n in range(w_ref.shape[1] // col_tile):
        cols = slice(n * col_tile, (n + 1) * col_tile)
        o_ref[:, cols] = _dot(h_scr[...], w_ref[:, cols]).astype(BF16)


def _inproj(x2, norm_w, w_in_b, layer, w_gate, *, n_main, gate_rows, row_tile=512, col_tile=512):
    m, d = x2.shape
    return pl.pallas_call(
        functools.partial(_inproj_body, col_tile=col_tile),
        grid=(m // row_tile,),
        in_specs=[
            pl.BlockSpec((row_tile, d), lambda i: (i, 0)),
            pl.BlockSpec((1, d), lambda i: (0, 0)),
            pl.BlockSpec((None, d, n_main), lambda i: (layer, 0, 0), pipeline_mode=pl.Buffered(1)),
            pl.BlockSpec((gate_rows, d), lambda i: (0, 0)),
        ],
        out_specs=[
            pl.BlockSpec((row_tile, n_main), lambda i: (i, 0)),
            pl.BlockSpec((gate_rows, row_tile), lambda i: (0, i)),
        ],
        out_shape=[
            jax.ShapeDtypeStruct((m, n_main), BF16),
            jax.ShapeDtypeStruct((gate_rows, m), F32),
        ],
        scratch_shapes=[pltpu.VMEM((row_tile, d), BF16)],
        compiler_params=pltpu.CompilerParams(
            dimension_semantics=("arbitrary",), vmem_limit_bytes=48 * MIB),
        name="inproj",
    )(x2, norm_w, w_in_b, w_gate)


POOL_TILE = 256
POOL_HALO = 16


def _pool_body(u_ref, z_ref, pw_ref, ps_ref, o_ref):
    seq = u_ref.shape[0]
    g = pl.program_id(1)
    left = jnp.left_shift(jnp.int32(1), g)
    right = left - 1
    i = lax.broadcasted_iota(jnp.int32, (POOL_TILE, POOL_TILE), 0)
    j = lax.broadcasted_iota(jnp.int32, (POOL_TILE, POOL_TILE), 1)
    band = ((j >= i - left) & (j <= i + right)).astype(BF16)
    ih = lax.broadcasted_iota(jnp.int32, (POOL_TILE, 2 * POOL_HALO), 0)
    ph = lax.broadcasted_iota(jnp.int32, (POOL_TILE, 2 * POOL_HALO), 1)
    in_prev = (ph < POOL_HALO) & (ph - POOL_HALO >= ih - left)
    in_next = (ph >= POOL_HALO) & (ph - POOL_HALO + POOL_TILE <= ih + right)
    halo_band = (in_prev | in_next).astype(BF16)
    pw = pw_ref[...]
    ps = ps_ref[...]
    n_tiles = seq // POOL_TILE
    tok0 = lax.broadcasted_iota(jnp.int32, (POOL_TILE, u_ref.shape[1]), 0)

    def inv_count(t0):
        tok = tok0 + t0
        cnt = jnp.minimum(tok + right, seq - 1) - jnp.maximum(tok - left, 0) + 1
        return 1.0 / cnt.astype(F32)

    inv_first, inv_mid, inv_last = inv_count(0), inv_count(POOL_TILE), inv_count(seq - POOL_TILE)
    zero_halo = jnp.zeros((POOL_HALO, u_ref.shape[1]), BF16)

    def window_sums(t):
        t0 = t * POOL_TILE
        prev = u_ref[t0 - POOL_HALO:t0, :] if t > 0 else zero_halo
        nxt = u_ref[t0 + POOL_TILE:t0 + POOL_TILE + POOL_HALO, :] if t < n_tiles - 1 else zero_halo
        return _dot(band, u_ref[t0:t0 + POOL_TILE, :]) + _dot(halo_band, jnp.concatenate([prev, nxt], axis=0))

    wsum = window_sums(0)
    for t in range(n_tiles):
        t0 = t * POOL_TILE
        wsum_next = window_sums(t + 1) if t < n_tiles - 1 else None
        inv = inv_first if t == 0 else (inv_last if t == n_tiles - 1 else inv_mid)
        m = wsum * inv - u_ref[t0:t0 + POOL_TILE, :].astype(F32)
        y = _dot(m.astype(BF16), pw) * ps
        o_ref[t0:t0 + POOL_TILE, :] = (y * _silu(z_ref[t0:t0 + POOL_TILE, :].astype(F32))).astype(BF16)
        wsum = wsum_next


def _pool(proj, pool_w, pool_scale, *, batch, seq):
    pool_g = pool_w.shape[-1]
    return pl.pallas_call(
        _pool_body,
        grid=(batch, N_POOL_GROUPS),
        in_specs=[
            pl.BlockSpec((seq, pool_g), lambda b, g: (b, g)),
            pl.BlockSpec((seq, pool_g), lambda b, g: (b, N_POOL_GROUPS + g)),
            pl.BlockSpec((None, pool_g, pool_g), lambda b, g: (g, 0, 0)),
            pl.BlockSpec((None, 1, pool_g), lambda b, g: (g, 0, 0)),
        ],
        out_specs=pl.BlockSpec((seq, pool_g), lambda b, g: (b, g)),
        out_shape=jax.ShapeDtypeStruct((batch * seq, N_POOL_GROUPS * pool_g), BF16),
        compiler_params=pltpu.CompilerParams(
            dimension_semantics=("arbitrary", "arbitrary"), vmem_limit_bytes=40 * MIB),
        name="pool",
    )(proj, proj, pool_w, pool_scale)


CONV_TILE = 512
CONV_PAD = 16
REP_TILES_PER_TASK = 8


def _seg_scan(x, pos, *, suffix):
    n = x.shape[1]
    s = 1
    while s < CHUNK:
        if suffix:
            x = x + jnp.where(pos < CHUNK - s, pltpu.roll(x, n - s, axis=1), 0.0)
        else:
            x = x + jnp.where(pos >= s, pltpu.roll(x, s, axis=1), 0.0)
        s *= 2
    return x


def _dn_body(alog_ref, dtb_ref,
             q0_ref, k0_ref, v0_ref, gl0_ref, cwq0_ref, cwk0_ref, cwv0_ref,
             qn_ref, kn_ref, vn_ref, gln_ref, cwqn_ref, cwkn_ref, cwvn_ref,
             z_ref, hnw_ref,
             o_ref,
             xpad, qh, kh, vh, kt, gc_row, er_row, beta_row, rep, kq_scr, aq_scr, b_scr, s_scr, o_acc,
             *, n_heads, n_steps):
    seq = z_ref.shape[0]
    n_chunks = seq // CHUNK
    n_groups = seq // GROUP
    n_trips = n_groups // GROUP_UNROLL
    chunks_per_trip = GROUP_UNROLL * CHUNKS_PER_GROUP
    step = pl.program_id(0) * n_heads + pl.program_id(1)
    slot = lax.rem(step, 2)
    next_head = lax.rem(jnp.minimum(step + 1, n_steps - 1), n_heads)

    def prep_tasks(q_ref, k_ref, v_ref, gl_ref, cw_refs, head, w):
        tasks = []

        def zero_pads():
            xpad[0:CONV_PAD, :] = jnp.zeros((CONV_PAD, HEAD_DIM), F32)
            xpad[CONV_PAD + seq:CONV_PAD + seq + CONV_PAD, :] = jnp.zeros((CONV_PAD, HEAD_DIM), F32)

        tasks.append(zero_pads)
        for which, (src, cw_ref) in enumerate(zip((q_ref, k_ref, v_ref), cw_refs)):
            def fill(src=src):
                for t in range(seq // CONV_TILE):
                    t0 = t * CONV_TILE
                    xpad[CONV_PAD + t0:CONV_PAD + t0 + CONV_TILE, :] = src[t0:t0 + CONV_TILE, :].astype(F32)

            tasks.append(fill)
            for t in range(seq // CONV_TILE):
                def conv(which=which, cw_ref=cw_ref, t0=t * CONV_TILE):
                    acc = jnp.zeros((CONV_TILE, HEAD_DIM), F32)
                    for tap in range(CONV_K):
                        lo = CONV_PAD + t0 + tap - CONV_K // 2
                        acc = acc + cw_ref[tap:tap + 1, :] * xpad[lo:lo + CONV_TILE, :]
                    s = _silu(acc)
                    rows = slice(t0, t0 + CONV_TILE)
                    if which == 0:
                        s = s * (lax.rsqrt(jnp.sum(s * s, axis=-1, keepdims=True) + EPS) * (HEAD_DIM ** -0.5))
                        qh[w, rows, :] = s.astype(BF16)
                    elif which == 1:
                        s = s * lax.rsqrt(jnp.sum(s * s, axis=-1, keepdims=True) + EPS)
                        kh[w, rows, :] = s.astype(BF16)
                        kt[w, :, rows] = s.T.astype(BF16)
                    else:
                        vh[w, rows, :] = s.astype(BF16)

                tasks.append(conv)

        def gates():
            r = gl_ref[...]
            rowi = lax.broadcasted_iota(jnp.int32, r.shape, 0)
            pos = lax.broadcasted_iota(jnp.int32, r.shape, 1) & (CHUNK - 1)
            fwd_row = rowi == 0
            dtb = jnp.where(fwd_row, dtb_ref[0, head], dtb_ref[1, head])
            a_coef = jnp.exp(jnp.where(fwd_row, alog_ref[0, head], alog_ref[1, head]))
            g = jnp.where(rowi < 2, -a_coef * jax.nn.softplus(r + dtb), 0.0)
            pre = _seg_scan(g, pos, suffix=False)
            suf = _seg_scan(g, pos, suffix=True)
            gc = jnp.where(fwd_row, pre, suf)
            tot = pre + suf - g
            gc_row[w] = gc
            er_row[w] = jnp.exp(tot - gc)
            beta_row[w] = _sigmoid(r)

        tasks.append(gates)
        for d in range(2):
            for t8 in range(0, seq // LANES, REP_TILES_PER_TASK):
                def replicate(d=d, t8=t8):
                    for t in range(t8, t8 + REP_TILES_PER_TASK):
                        cols = slice(t * LANES, (t + 1) * LANES)
                        tile = jnp.broadcast_to(gc_row[w, d:d + 1, cols], (LANES, LANES))
                        rep[w, d, cols, :] = tile.T

                tasks.append(replicate)
        return tasks

    @pl.when(step == 0)
    def _():
        for task in prep_tasks(q0_ref, k0_ref, v0_ref, gl0_ref, (cwq0_ref, cwk0_ref, cwv0_ref), 0, 0):
            task()

    next_tasks = prep_tasks(qn_ref, kn_ref, vn_ref, gln_ref, (cwqn_ref, cwkn_ref, cwvn_ref), next_head, 1 - slot)

    ci = lax.broadcasted_iota(jnp.int32, (CHUNK, GROUP), 0)
    cl = lax.broadcasted_iota(jnp.int32, (CHUNK, GROUP), 1)
    cj = cl & (CHUNK - 1)
    lane_chunk = cl // CHUNK
    eye_c = (cj == ci).astype(F32)
    first_half = lax.broadcasted_iota(jnp.int32, (CHUNK, LANES), 1) < CHUNK
    bi = lax.broadcasted_iota(jnp.int32, (GROUP, GROUP), 0)
    bj = lax.broadcasted_iota(jnp.int32, (GROUP, GROUP), 1)
    bd_mask = ((bi // CHUNK) == (bj // CHUNK)).astype(F32).astype(BF16)

    def compact(full):
        out = full[(CHUNKS_PER_GROUP - 1) * CHUNK:, :]
        for c in range(CHUNKS_PER_GROUP - 2, -1, -1):
            out = jnp.where(lane_chunk == c, full[c * CHUNK:(c + 1) * CHUNK, :], out)
        return out

    def per_chunk_lanes(tok):
        blk = [tok[c * CHUNK:(c + 1) * CHUNK, :] for c in range(CHUNKS_PER_GROUP)]
        return jnp.concatenate([jnp.where(first_half, blk[0], blk[1]), jnp.where(first_half, blk[2], blk[3])], axis=1)

    def block_diag(cmp):
        cb = cmp.astype(BF16)
        return jnp.concatenate([cb] * CHUNKS_PER_GROUP, axis=0) * bd_mask

    def chunk_lanes(x, c):
        half = x[:, (c // 2) * LANES:(c // 2 + 1) * LANES]
        if c % 2:
            half = pltpu.roll(half, CHUNK, axis=1)
        return half[:, :CHUNK]

    o_acc[...] = jnp.zeros(o_acc.shape, F32)

    def recurrence_step(states, k, trip):
        for d in range(2):
            lin = trip * chunks_per_trip + k
            c = lin if d == 0 else n_chunks - 1 - lin
            r0 = c * CHUNK
            if not isinstance(r0, int):
                r0 = pl.multiple_of(r0, CHUNK)
            res = _dot(aq_scr[trip % 2, d, k], states[d].astype(BF16))
            last = r0 + (CHUNK - 1 if d == 0 else 0)
            dec = jnp.exp(rep[slot, d, pl.ds(last, 1), :])
            states[d] = states[d] * dec - res[:HEAD_DIM] + b_scr[trip % 2, d, k]
            o_acc[pl.ds(r0, CHUNK), :] += res[HEAD_DIM:]

    def precompute(trip, tick):
        chains = []
        for d in range(2):
            incl = (cj <= ci) if d == 0 else (cj >= ci)
            strict = (cj < ci) if d == 0 else (cj > ci)
            for k in range(GROUP_UNROLL):
                lin = trip * GROUP_UNROLL + k
                gi = lin if d == 0 else n_groups - 1 - lin
                rows = slice(gi * GROUP, (gi + 1) * GROUP)
                if trip < n_trips // 2:
                    kt_g = kt[slot, :, rows]
                    kk = compact(_dot(kh[slot, rows, :], kt_g))
                    tick()
                    qk = compact(_dot(qh[slot, rows, :], kt_g))
                    tick()
                    kq_scr[gi, 0] = kk
                    kq_scr[gi, 1] = qk
                else:
                    kk = kq_scr[gi, 0]
                    qk = kq_scr[gi, 1]
                diff = per_chunk_lanes(rep[slot, d, rows, :]) - gc_row[slot, d:d + 1, rows]
                decay = jnp.where(incl, jnp.exp(jnp.where(incl, diff, 0.0)), 0.0)
                decay_beta = decay * beta_row[slot, 2 + d:3 + d, rows]
                mk = jnp.where(strict, -(kk * decay_beta), 0.0)
                a = jnp.where(incl, qk * decay_beta, 0.0)
                chains.append(dict(k=k, r0=gi * GROUP, rows=rows, d=d, mk=mk, tinv=eye_c + mk, a=a))
        for ch in chains:
            ch["mk"] = _dot(ch["mk"].astype(BF16), block_diag(ch["mk"]))
            tick()
        p = 2
        while p < CHUNK // 2:
            for ch in chains:
                x2 = _dot(jnp.concatenate([ch["tinv"], ch["mk"]], axis=0).astype(BF16), block_diag(ch["mk"]))
                tick()
                ch["tinv"] = ch["tinv"] + x2[:CHUNK]
                ch["mk"] = x2[CHUNK:]
            p *= 2
        for ch in chains:
            ch["tinv"] = ch["tinv"] + _dot(ch["tinv"].astype(BF16), block_diag(ch["mk"]))
            tick()
        for ch in chains:
            d, rows = ch["d"], ch["rows"]
            ke = (kh[slot, rows, :].astype(F32) * jnp.exp(rep[slot, d, rows, :])).astype(BF16)
            rhs = jnp.concatenate([ke, vh[slot, rows, :]], axis=1)
            ch["wu"] = _dot(block_diag(ch["tinv"]), rhs).astype(BF16)
            tick()
        for ch in chains:
            d, rows = ch["d"], ch["rows"]
            q_til = qh[slot, rows, :].astype(F32) * jnp.exp(rep[slot, d, rows, :])
            kt_til = kt[slot, :, rows].astype(F32) * (er_row[slot, d:d + 1, rows] * beta_row[slot, 2 + d:3 + d, rows])
            lhs_all = jnp.concatenate([kt_til, ch["a"]], axis=0)
            for c in range(CHUNKS_PER_GROUP):
                crow = slice(c * CHUNK, (c + 1) * CHUNK)
                res = _dot(chunk_lanes(lhs_all, c).astype(BF16), ch["wu"][crow, :])
                tick()
                loc = ch["k"] * CHUNKS_PER_GROUP + (c if d == 0 else CHUNKS_PER_GROUP - 1 - c)
                aq_scr[trip % 2, d, loc, 0:HEAD_DIM, :] = res[:HEAD_DIM, :HEAD_DIM].astype(BF16)
                aq_scr[trip % 2, d, loc, HEAD_DIM:, :] = (q_til[crow, :] - res[HEAD_DIM:, :HEAD_DIM]).astype(BF16)
                b_scr[trip % 2, d, loc] = res[:HEAD_DIM, HEAD_DIM:]
                o_acc[ch["r0"] + c * CHUNK:ch["r0"] + (c + 1) * CHUNK, :] += res[HEAD_DIM:, HEAD_DIM:]

    states = [jnp.zeros((HEAD_DIM, HEAD_DIM), F32), jnp.zeros((HEAD_DIM, HEAD_DIM), F32)]
    progress = dict(dots=0, prep=0)
    for trip in range(n_trips):
        progress["steps"] = 0 if trip > 0 else chunks_per_trip

        def tick(trip=trip):
            progress["dots"] += 1
            if progress["dots"] % REC_EVERY == 0 and progress["steps"] < chunks_per_trip:
                recurrence_step(states, progress["steps"], trip - 1)
                progress["steps"] += 1
            if progress["dots"] % PREP_EVERY == 0 and progress["prep"] < len(next_tasks):
                next_tasks[progress["prep"]]()
                progress["prep"] += 1

        precompute(trip, tick)
        while progress["steps"] < chunks_per_trip:
            recurrence_step(states, progress["steps"], trip - 1)
            progress["steps"] += 1
    for task in next_tasks[progress["prep"]:]:
        task()
    s_scr[0] = states[0]
    s_scr[1] = states[1]

    hnw = hnw_ref[...]

    def finish_rows(r0, n):
        rows = pl.ds(r0, n)
        o = o_acc[rows, :]
        y = o * lax.rsqrt(jnp.mean(o * o, axis=-1, keepdims=True) + EPS) * hnw
        o_ref[rows, :] = (y * _silu(z_ref[rows, :].astype(F32))).astype(BF16)

    front = (n_trips - 1) * chunks_per_trip * CHUNK
    back = seq - front
    mid_rows = (front - back) // chunks_per_trip

    def drain_step(k, carry):
        finish_rows(pl.multiple_of(back + k * mid_rows, mid_rows), mid_rows)
        prev = jnp.maximum(k - 1, 0)
        finish_rows(pl.multiple_of(front + prev * CHUNK, CHUNK), CHUNK)
        finish_rows(pl.multiple_of(back - (prev + 1) * CHUNK, CHUNK), CHUNK)
        drain_states = [s_scr[0], s_scr[1]]
        recurrence_step(drain_states, k, n_trips - 1)
        s_scr[0] = drain_states[0]
        s_scr[1] = drain_states[1]
        return carry

    lax.fori_loop(0, chunks_per_trip, drain_step, 0)
    finish_rows(seq - CHUNK, CHUNK)
    finish_rows(0, CHUNK)


def _deltanet(proj, gate_rows, conv_w3, a_log, dt_bias, head_norm_w, *, batch, seq, n_heads, qkv_block0, z_block0):
    n_steps = batch * n_heads
    chunks_per_trip = GROUP_UNROLL * CHUNKS_PER_GROUP
    smem = pl.BlockSpec(memory_space=pltpu.SMEM)

    def next_step(b, h):
        lin = jnp.minimum(b * n_heads + h + 1, n_steps - 1)
        return lin // n_heads, lin % n_heads

    def first_cols(off):
        return pl.BlockSpec((seq, HEAD_DIM), lambda b, h: (0, off), pipeline_mode=pl.Buffered(1))

    def next_cols(off):
        def index(b, h):
            nb, nh = next_step(b, h)
            return nb, off + nh
        return pl.BlockSpec((seq, HEAD_DIM), index)

    def next_gates(b, h):
        nb, nh = next_step(b, h)
        return nh, 0, nb

    def next_conv(off):
        return pl.BlockSpec((None, 8, HEAD_DIM), lambda b, h: (off + next_step(b, h)[1], 0, 0))

    first_conv = lambda off: pl.BlockSpec((None, 8, HEAD_DIM), lambda b, h: (off, 0, 0))
    k_off, v_off = qkv_block0 + n_heads, qkv_block0 + 2 * n_heads
    return pl.pallas_call(
        functools.partial(_dn_body, n_heads=n_heads, n_steps=n_steps),
        grid=(batch, n_heads),
        in_specs=[
            smem, smem,
            first_cols(qkv_block0), first_cols(k_off), first_cols(v_off),
            pl.BlockSpec((None, GATE_ROWS, seq), lambda b, h: (0, 0, 0)),
            first_conv(0), first_conv(n_heads), first_conv(2 * n_heads),
            next_cols(qkv_block0), next_cols(k_off), next_cols(v_off),
            pl.BlockSpec((None, GATE_ROWS, seq), next_gates),
            next_conv(0), next_conv(n_heads), next_conv(2 * n_heads),
            pl.BlockSpec((seq, HEAD_DIM), lambda b, h: (b, z_block0 + h)),
            pl.BlockSpec((1, HEAD_DIM), lambda b, h: (0, 0)),
        ],
        out_specs=pl.BlockSpec((seq, HEAD_DIM), lambda b, h: (b, h)),
        out_shape=jax.ShapeDtypeStruct((batch * seq, n_heads * HEAD_DIM), BF16),
        scratch_shapes=[
            pltpu.VMEM((seq + 2 * CONV_PAD, HEAD_DIM), F32),
            pltpu.VMEM((2, seq, HEAD_DIM), BF16),
            pltpu.VMEM((2, seq, HEAD_DIM), BF16),
            pltpu.VMEM((2, seq, HEAD_DIM), BF16),
            pltpu.VMEM((2, HEAD_DIM, seq), BF16),
            pltpu.VMEM((2, GATE_ROWS, seq), F32),
            pltpu.VMEM((2, GATE_ROWS, seq), F32),
            pltpu.VMEM((2, GATE_ROWS, seq), F32),
            pltpu.VMEM((2, 2, seq, LANES), F32),
            pltpu.VMEM((seq // GROUP, 2, CHUNK, GROUP), F32),
            pltpu.VMEM((2, 2, chunks_per_trip, HEAD_DIM + CHUNK, HEAD_DIM), BF16),
            pltpu.VMEM((2, 2, chunks_per_trip, HEAD_DIM, HEAD_DIM), F32),
            pltpu.VMEM((2, HEAD_DIM, HEAD_DIM), F32),
            pltpu.VMEM((seq, HEAD_DIM), F32),
        ],
        compiler_params=pltpu.CompilerParams(
            dimension_semantics=("arbitrary", "arbitrary"), vmem_limit_bytes=62 * MIB),
        name="deltanet",
    )(a_log, dt_bias,
      proj, proj, proj, gate_rows, conv_w3, conv_w3, conv_w3,
      proj, proj, proj, gate_rows, conv_w3, conv_w3, conv_w3,
      proj, head_norm_w)


def _outproj_body(yp_ref, yd_ref, x_ref, w1_ref, w2_ref, fnw_ref, o_ref, *, final):
    acc = x_ref[...] + _dot(yp_ref[...], w1_ref[...].astype(BF16)) + _dot(yd_ref[...], w2_ref[...].astype(BF16))
    if final:
        acc = acc * lax.rsqrt(jnp.mean(acc * acc, axis=-1, keepdims=True) + EPS) * fnw_ref[...]
    o_ref[...] = acc


def _outproj(y_pool, y_dn, x2, w_out, layer, final_norm_w, *, final, row_tile=512):
    m, d = x2.shape
    k1, k2 = y_pool.shape[1], y_dn.shape[1]
    assert k1 == k2
    return pl.pallas_call(
        functools.partial(_outproj_body, final=final),
        grid=(m // row_tile,),
        in_specs=[
            pl.BlockSpec((row_tile, k1), lambda i: (i, 0)),
            pl.BlockSpec((row_tile, k2), lambda i: (i, 0)),
            pl.BlockSpec((row_tile, d), lambda i: (i, 0)),
            pl.BlockSpec((None, k1, d), lambda i: (layer, 0, 0), pipeline_mode=pl.Buffered(1)),
            pl.BlockSpec((None, k2, d), lambda i: (layer, 1, 0), pipeline_mode=pl.Buffered(1)),
            pl.BlockSpec((1, d), lambda i: (0, 0)),
        ],
        out_specs=pl.BlockSpec((row_tile, d), lambda i: (i, 0)),
        out_shape=jax.ShapeDtypeStruct((m, d), F32),
        compiler_params=pltpu.CompilerParams(
            dimension_semantics=("arbitrary",), vmem_limit_bytes=40 * MIB),
        name="outproj",
    )(y_pool, y_dn, x2, w_out, w_out, final_norm_w)


def kernel(x, norm_w, w_in, pool_w, pool_scale, conv_w, a_log, dt_bias, head_norm_w, w_out, final_norm_w):
    batch, seq, d_model = x.shape
    depth = norm_w.shape[0]
    pool_g = pool_w.shape[-1]
    pool_width = N_POOL_GROUPS * pool_g
    n_dirs, n_heads = a_log.shape[1], a_log.shape[2]
    dn_width = n_heads * HEAD_DIM
    n_main = 2 * pool_width + 4 * dn_width
    n_gate = 2 * n_dirs * n_heads
    qkv_block0 = 2 * pool_width // HEAD_DIM
    z_block0 = qkv_block0 + 3 * n_heads

    x2 = x.reshape(batch * seq, d_model)
    w_in_b = w_in.astype(BF16)
    for l in range(depth):
        w_gate = w_in[l, :, n_main:n_main + n_gate].reshape(d_model, 2, n_dirs, n_heads)[:, ::-1]
        w_gate = jnp.transpose(w_gate, (0, 3, 1, 2)).reshape(d_model, n_heads, 2 * n_dirs)
        w_gate = jnp.pad(w_gate, ((0, 0), (0, 0), (0, GATE_ROWS - 2 * n_dirs))).reshape(d_model, n_heads * GATE_ROWS)
        proj, gate_t = _inproj(x2, norm_w[l][None, :], w_in_b, l, w_gate.T.astype(BF16),
                               n_main=n_main, gate_rows=n_heads * GATE_ROWS)
        gate_rows = gate_t.reshape(n_heads, GATE_ROWS, batch * seq)
        conv_w3 = jnp.pad(conv_w[l], ((0, 8 - CONV_K), (0, 0))).reshape(8, 3 * n_heads, HEAD_DIM)
        conv_w3 = jnp.transpose(conv_w3, (1, 0, 2))
        y_pool = _pool(proj, pool_w[l].astype(BF16), pool_scale[l].reshape(N_POOL_GROUPS, 1, pool_g),
                       batch=batch, seq=seq)
        y_dn = _deltanet(proj, gate_rows, conv_w3, a_log[l], dt_bias[l], head_norm_w[l][None, :],
                         batch=batch, seq=seq, n_heads=n_heads, qkv_block0=qkv_block0, z_block0=z_block0)
        x2 = _outproj(y_pool, y_dn, x2, w_out, l, final_norm_w[None, :], final=(l == depth - 1))
    return x2.reshape(batch, seq, d_model)
```

```python
import functools

import jax
import jax.numpy as jnp
from jax import lax
from jax.experimental import pallas as pl
from jax.experimental.pallas import tpu as pltpu

F32 = jnp.float32
BF16 = jnp.bfloat16

EPS = 1e-6
POOL_WINDOWS = (2, 4, 8, 16)
N_POOL_GROUPS = len(POOL_WINDOWS)
HEAD_DIM = 128
CONV_K = 5
CHUNK = 64
GROUP = 256
CHUNKS_PER_GROUP = GROUP // CHUNK
GROUP_UNROLL = 4
REC_EVERY = 6
PREP_EVERY = 10
GATE_ROWS = 8
LANES = 128
MIB = 1024 * 1024


def _sigmoid(x):
    return 0.5 * jnp.tanh(0.5 * x) + 0.5


def _silu(x):
    half = 0.5 * x
    return half + half * jnp.tanh(half)


def _dot(a, b):
    return jnp.dot(a, b, preferred_element_type=F32)


def _inproj_body(x_ref, nw_ref, w_ref, wg_ref, o_ref, g_ref, h_scr, *, col_tile):
    x = x_ref[...]
    ms = jnp.mean(x * x, axis=-1, keepdims=True)
    h_scr[...] = (x * lax.rsqrt(ms + EPS) * nw_ref[...]).astype(BF16)
    g_ref[...] = lax.dot_general(wg_ref[...], h_scr[...], (((1,), (1,)), ((), ())), preferred_element_type=F32)
    for n in range(w_ref.shape[1] // col_tile):
        cols = slice(n * col_tile, (n + 1) * col_tile)
        o_ref[:, cols] = _dot(h_scr[...], w_ref[:, cols]).astype(BF16)


def _inproj(x2, norm_w, w_in_b, layer, w_gate, *, n_main, gate_rows, row_tile=1024, col_tile=512):
    m, d = x2.shape
    return pl.pallas_call(
        functools.partial(_inproj_body, col_tile=col_tile),
        grid=(m // row_tile,),
        in_specs=[
            pl.BlockSpec((row_tile, d), lambda i: (i, 0)),
            pl.BlockSpec((1, d), lambda i: (0, 0)),
            pl.BlockSpec((None, d, n_main), lambda i: (layer, 0, 0), pipeline_mode=pl.Buffered(1)),
            pl.BlockSpec((gate_rows, d), lambda i: (0, 0)),
        ],
        out_specs=[
            pl.BlockSpec((row_tile, n_main), lambda i: (i, 0)),
            pl.BlockSpec((gate_rows, row_tile), lambda i: (0, i)),
        ],
        out_shape=[
            jax.ShapeDtypeStruct((m, n_main), BF16),
            jax.ShapeDtypeStruct((gate_rows, m), F32),
        ],
        scratch_shapes=[pltpu.VMEM((row_tile, d), BF16)],
        compiler_params=pltpu.CompilerParams(
            dimension_semantics=("arbitrary",), vmem_limit_bytes=56 * MIB),
        name="inproj",
    )(x2, norm_w, w_in_b, w_gate)


POOL_TILE = 256
POOL_HALO = 16


def _pool_body(u_ref, z_ref, pw_ref, ps_ref, o_ref):
    seq = u_ref.shape[0]
    g = pl.program_id(1)
    left = jnp.left_shift(jnp.int32(1), g)
    right = left - 1
    i = lax.broadcasted_iota(jnp.int32, (POOL_TILE, POOL_TILE), 0)
    j = lax.broadcasted_iota(jnp.int32, (POOL_TILE, POOL_TILE), 1)
    band = ((j >= i - left) & (j <= i + right)).astype(BF16)
    ih = lax.broadcasted_iota(jnp.int32, (POOL_TILE, 2 * POOL_HALO), 0)
    ph = lax.broadcasted_iota(jnp.int32, (POOL_TILE, 2 * POOL_HALO), 1)
    in_prev = (ph < POOL_HALO) & (ph - POOL_HALO >= ih - left)
    in_next = (ph >= POOL_HALO) & (ph - POOL_HALO + POOL_TILE <= ih + right)
    halo_band = (in_prev | in_next).astype(BF16)
    pw = pw_ref[...]
    ps = ps_ref[...]
    n_tiles = seq // POOL_TILE
    tok0 = lax.broadcasted_iota(jnp.int32, (POOL_TILE, u_ref.shape[1]), 0)

    def inv_count(t0):
        tok = tok0 + t0
        cnt = jnp.minimum(tok + right, seq - 1) - jnp.maximum(tok - left, 0) + 1
        return 1.0 / cnt.astype(F32)

    inv_first, inv_mid, inv_last = inv_count(0), inv_count(POOL_TILE), inv_count(seq - POOL_TILE)
    zero_halo = jnp.zeros((POOL_HALO, u_ref.shape[1]), BF16)

    def window_sums(t):
        t0 = t * POOL_TILE
        prev = u_ref[t0 - POOL_HALO:t0, :] if t > 0 else zero_halo
        nxt = u_ref[t0 + POOL_TILE:t0 + POOL_TILE + POOL_HALO, :] if t < n_tiles - 1 else zero_halo
        return _dot(band, u_ref[t0:t0 + POOL_TILE, :]) + _dot(halo_band, jnp.concatenate([prev, nxt], axis=0))

    wsum = window_sums(0)
    for t in range(n_tiles):
        t0 = t * POOL_TILE
        wsum_next = window_sums(t + 1) if t < n_tiles - 1 else None
        inv = inv_first if t == 0 else (inv_last if t == n_tiles - 1 else inv_mid)
        m = wsum * inv - u_ref[t0:t0 + POOL_TILE, :].astype(F32)
        y = _dot(m.astype(BF16), pw) * ps
        o_ref[t0:t0 + POOL_TILE, :] = (y * _silu(z_ref[t0:t0 + POOL_TILE, :].astype(F32))).astype(BF16)
        wsum = wsum_next


def _pool(proj, pool_w, pool_scale, *, batch, seq):
    pool_g = pool_w.shape[-1]
    return pl.pallas_call(
        _pool_body,
        grid=(batch, N_POOL_GROUPS),
        in_specs=[
            pl.BlockSpec((seq, pool_g), lambda b, g: (b, g)),
            pl.BlockSpec((seq, pool_g), lambda b, g: (b, N_POOL_GROUPS + g)),
            pl.BlockSpec((None, pool_g, pool_g), lambda b, g: (g, 0, 0)),
            pl.BlockSpec((None, 1, pool_g), lambda b, g: (g, 0, 0)),
        ],
        out_specs=pl.BlockSpec((seq, pool_g), lambda b, g: (b, g)),
        out_shape=jax.ShapeDtypeStruct((batch * seq, N_POOL_GROUPS * pool_g), BF16),
        compiler_params=pltpu.CompilerParams(
            dimension_semantics=("arbitrary", "arbitrary"), vmem_limit_bytes=40 * MIB),
        name="pool",
    )(proj, proj, pool_w, pool_scale)


CONV_TILE = 512
CONV_PAD = 16
REP_TILES_PER_TASK = 8


def _seg_scan(x, pos, *, suffix):
    n = x.shape[1]
    s = 1
    while s < CHUNK:
        if suffix:
            x = x + jnp.where(pos < CHUNK - s, pltpu.roll(x, n - s, axis=1), 0.0)
        else:
            x = x + jnp.where(pos >= s, pltpu.roll(x, s, axis=1), 0.0)
        s *= 2
    return x


def _dn_body(alog_ref, dtb_ref,
             q0_ref, k0_ref, v0_ref, gl0_ref, cwq0_ref, cwk0_ref, cwv0_ref,
             qn_ref, kn_ref, vn_ref, gln_ref, cwqn_ref, cwkn_ref, cwvn_ref,
             z_ref, hnw_ref,
             o_ref,
             xpad, qh, kh, vh, kt, gc_row, er_row, beta_row, rep, kq_scr, aq_scr, b_scr, s_scr, o_acc,
             *, n_heads, n_steps):
    seq = z_ref.shape[0]
    n_chunks = seq // CHUNK
    n_groups = seq // GROUP
    n_trips = n_groups // GROUP_UNROLL
    chunks_per_trip = GROUP_UNROLL * CHUNKS_PER_GROUP
    step = pl.program_id(0) * n_heads + pl.program_id(1)
    slot = lax.rem(step, 2)
    next_head = lax.rem(jnp.minimum(step + 1, n_steps - 1), n_heads)

    def prep_tasks(q_ref, k_ref, v_ref, gl_ref, cw_refs, head, w):
        tasks = []

        def zero_pads():
            xpad[0:CONV_PAD, :] = jnp.zeros((CONV_PAD, HEAD_DIM), F32)
            xpad[CONV_PAD + seq:CONV_PAD + seq + CONV_PAD, :] = jnp.zeros((CONV_PAD, HEAD_DIM), F32)

        tasks.append(zero_pads)
        for which, (src, cw_ref) in enumerate(zip((q_ref, k_ref, v_ref), cw_refs)):
            def fill(src=src):
                for t in range(seq // CONV_TILE):
                    t0 = t * CONV_TILE
                    xpad[CONV_PAD + t0:CONV_PAD + t0 + CONV_TILE, :] = src[t0:t0 + CONV_TILE, :].astype(F32)

            tasks.append(fill)
            for t in range(seq // CONV_TILE):
                def conv(which=which, cw_ref=cw_ref, t0=t * CONV_TILE):
                    acc = jnp.zeros((CONV_TILE, HEAD_DIM), F32)
                    for tap in range(CONV_K):
                        lo = CONV_PAD + t0 + tap - CONV_K // 2
                        acc = acc + cw_ref[tap:tap + 1, :] * xpad[lo:lo + CONV_TILE, :]
                    s = _silu(acc)
                    rows = slice(t0, t0 + CONV_TILE)
                    if which == 0:
                        s = s * (lax.rsqrt(jnp.sum(s * s, axis=-1, keepdims=True) + EPS) * (HEAD_DIM ** -0.5))
                        qh[w, rows, :] = s.astype(BF16)
                    elif which == 1:
                        s = s * lax.rsqrt(jnp.sum(s * s, axis=-1, keepdims=True) + EPS)
                        kh[w, rows, :] = s.astype(BF16)
                        kt[w, :, rows] = s.T.astype(BF16)
                    else:
                        vh[w, rows, :] = s.astype(BF16)

                tasks.append(conv)

        def gates():
            r = gl_ref[...]
            rowi = lax.broadcasted_iota(jnp.int32, r.shape, 0)
            pos = lax.broadcasted_iota(jnp.int32, r.shape, 1) & (CHUNK - 1)
            fwd_row = rowi == 0
            dtb = jnp.where(fwd_row, dtb_ref[0, head], dtb_ref[1, head])
            a_coef = jnp.exp(jnp.where(fwd_row, alog_ref[0, head], alog_ref[1, head]))
            g = jnp.where(rowi < 2, -a_coef * jax.nn.softplus(r + dtb), 0.0)
            pre = _seg_scan(g, pos, suffix=False)
            suf = _seg_scan(g, pos, suffix=True)
            gc = jnp.where(fwd_row, pre, suf)
            tot = pre + suf - g
            gc_row[w] = gc
            er_row[w] = jnp.exp(tot - gc)
            beta_row[w] = _sigmoid(r)

        tasks.append(gates)
        for d in range(2):
            for t8 in range(0, seq // LANES, REP_TILES_PER_TASK):
                def replicate(d=d, t8=t8):
                    for t in range(t8, t8 + REP_TILES_PER_TASK):
                        cols = slice(t * LANES, (t + 1) * LANES)
                        tile = jnp.broadcast_to(gc_row[w, d:d + 1, cols], (LANES, LANES))
                        rep[w, d, cols, :] = tile.T

                tasks.append(replicate)
        return tasks

    @pl.when(step == 0)
    def _():
        for task in prep_tasks(q0_ref, k0_ref, v0_ref, gl0_ref, (cwq0_ref, cwk0_ref, cwv0_ref), 0, 0):
            task()

    next_tasks = prep_tasks(qn_ref, kn_ref, vn_ref, gln_ref, (cwqn_ref, cwkn_ref, cwvn_ref), next_head, 1 - slot)

    ci = lax.broadcasted_iota(jnp.int32, (CHUNK, GROUP), 0)
    cl = lax.broadcasted_iota(jnp.int32, (CHUNK, GROUP), 1)
    cj = cl & (CHUNK - 1)
    lane_chunk = cl // CHUNK
    eye_c = (cj == ci).astype(F32)
    first_half = lax.broadcasted_iota(jnp.int32, (CHUNK, LANES), 1) < CHUNK
    bi = lax.broadcasted_iota(jnp.int32, (GROUP, GROUP), 0)
    bj = lax.broadcasted_iota(jnp.int32, (GROUP, GROUP), 1)
    bd_mask = ((bi // CHUNK) == (bj // CHUNK)).astype(F32).astype(BF16)

    def compact(full):
        out = full[(CHUNKS_PER_GROUP - 1) * CHUNK:, :]
        for c in range(CHUNKS_PER_GROUP - 2, -1, -1):
            out = jnp.where(lane_chunk == c, full[c * CHUNK:(c + 1) * CHUNK, :], out)
        return out

    def per_chunk_lanes(tok):
        blk = [tok[c * CHUNK:(c + 1) * CHUNK, :] for c in range(CHUNKS_PER_GROUP)]
        return jnp.concatenate([jnp.where(first_half, blk[0], blk[1]), jnp.where(first_half, blk[2], blk[3])], axis=1)

    def block_diag(cmp):
        cb = cmp.astype(BF16)
        return jnp.concatenate([cb] * CHUNKS_PER_GROUP, axis=0) * bd_mask

    def chunk_lanes(x, c):
        half = x[:, (c // 2) * LANES:(c // 2 + 1) * LANES]
        if c % 2:
            half = pltpu.roll(half, CHUNK, axis=1)
        return half[:, :CHUNK]

    o_acc[...] = jnp.zeros(o_acc.shape, F32)

    def recurrence_step(states, k, trip):
        for d in range(2):
            lin = trip * chunks_per_trip + k
            c = lin if d == 0 else n_chunks - 1 - lin
            r0 = c * CHUNK
            if not isinstance(r0, int):
                r0 = pl.multiple_of(r0, CHUNK)
            res = _dot(aq_scr[trip % 2, d, k], states[d].astype(BF16))
            last = r0 + (CHUNK - 1 if d == 0 else 0)
            dec = jnp.exp(rep[slot, d, pl.ds(last, 1), :])
            states[d] = states[d] * dec - res[:HEAD_DIM] + b_scr[trip % 2, d, k]
            o_acc[pl.ds(r0, CHUNK), :] += res[HEAD_DIM:]

    def precompute(trip, tick):
        chains = []
        for d in range(2):
            incl = (cj <= ci) if d == 0 else (cj >= ci)
            strict = (cj < ci) if d == 0 else (cj > ci)
            for k in range(GROUP_UNROLL):
                lin = trip * GROUP_UNROLL + k
                gi = lin if d == 0 else n_groups - 1 - lin
                rows = slice(gi * GROUP, (gi + 1) * GROUP)
                if trip < n_trips // 2:
                    kt_g = kt[slot, :, rows]
                    kk = compact(_dot(kh[slot, rows, :], kt_g))
                    tick()
                    qk = compact(_dot(qh[slot, rows, :], kt_g))
                    tick()
                    kq_scr[gi, 0] = kk
                    kq_scr[gi, 1] = qk
                else:
                    kk = kq_scr[gi, 0]
                    qk = kq_scr[gi, 1]
                diff = per_chunk_lanes(rep[slot, d, rows, :]) - gc_row[slot, d:d + 1, rows]
                decay = jnp.where(incl, jnp.exp(jnp.where(incl, diff, 0.0)), 0.0)
                decay_beta = decay * beta_row[slot, 2 + d:3 + d, rows]
                mk = jnp.where(strict, -(kk * decay_beta), 0.0)
                a = jnp.where(incl, qk * decay_beta, 0.0)
                chains.append(dict(k=k, r0=gi * GROUP, rows=rows, d=d, mk=mk, tinv=eye_c + mk, a=a))
        for ch in chains:
            ch["mk"] = _dot(ch["mk"].astype(BF16), block_diag(ch["mk"]))
            tick()
        p = 2
        while p < CHUNK // 2:
            for ch in chains:
                x2 = _dot(jnp.concatenate([ch["tinv"], ch["mk"]], axis=0).astype(BF16), block_diag(ch["mk"]))
                tick()
                ch["tinv"] = ch["tinv"] + x2[:CHUNK]
                ch["mk"] = x2[CHUNK:]
            p *= 2
        for ch in chains:
            ch["tinv"] = ch["tinv"] + _dot(ch["tinv"].astype(BF16), block_diag(ch["mk"]))
            tick()
        for ch in chains:
            d, rows = ch["d"], ch["rows"]
            ke = (kh[slot, rows, :].astype(F32) * jnp.exp(rep[slot, d, rows, :])).astype(BF16)
            rhs = jnp.concatenate([ke, vh[slot, rows, :]], axis=1)
            ch["wu"] = _dot(block_diag(ch["tinv"]), rhs).astype(BF16)
            tick()
        for ch in chains:
            d, rows = ch["d"], ch["rows"]
            q_til = qh[slot, rows, :].astype(F32) * jnp.exp(rep[slot, d, rows, :])
            kt_til = kt[slot, :, rows].astype(F32) * (er_row[slot, d:d + 1, rows] * beta_row[slot, 2 + d:3 + d, rows])
            lhs_all = jnp.concatenate([kt_til, ch["a"]], axis=0)
            for c in range(CHUNKS_PER_GROUP):
                crow = slice(c * CHUNK, (c + 1) * CHUNK)
                res = _dot(chunk_lanes(lhs_all, c).astype(BF16), ch["wu"][crow, :])
                tick()
                loc = ch["k"] * CHUNKS_PER_GROUP + (c if d == 0 else CHUNKS_PER_GROUP - 1 - c)
                aq_scr[trip % 2, d, loc, 0:HEAD_DIM, :] = res[:HEAD_DIM, :HEAD_DIM].astype(BF16)
                aq_scr[trip % 2, d, loc, HEAD_DIM:, :] = (q_til[crow, :] - res[HEAD_DIM:, :HEAD_DIM]).astype(BF16)
                b_scr[trip % 2, d, loc] = res[:HEAD_DIM, HEAD_DIM:]
                o_acc[ch["r0"] + c * CHUNK:ch["r0"] + (c + 1) * CHUNK, :] += res[HEAD_DIM:, HEAD_DIM:]

    states = [jnp.zeros((HEAD_DIM, HEAD_DIM), F32), jnp.zeros((HEAD_DIM, HEAD_DIM), F32)]
    progress = dict(dots=0, prep=0)
    for trip in range(n_trips):
        progress["steps"] = 0 if trip > 0 else chunks_per_trip

        def tick(trip=trip):
            progress["dots"] += 1
            if progress["dots"] % REC_EVERY == 0 and progress["steps"] < chunks_per_trip:
                recurrence_step(states, progress["steps"], trip - 1)
                progress["steps"] += 1
            if progress["dots"] % PREP_EVERY == 0 and progress["prep"] < len(next_tasks):
                next_tasks[progress["prep"]]()
                progress["prep"] += 1

        precompute(trip, tick)
        while progress["steps"] < chunks_per_trip:
            recurrence_step(states, progress["steps"], trip - 1)
            progress["steps"] += 1
    for task in next_tasks[progress["prep"]:]:
        task()
    s_scr[0] = states[0]
    s_scr[1] = states[1]

    hnw = hnw_ref[...]

    def finish_rows(r0, n):
        rows = pl.ds(r0, n)
        o = o_acc[rows, :]
        y = o * lax.rsqrt(jnp.mean(o * o, axis=-1, keepdims=True) + EPS) * hnw
        o_ref[rows, :] = (y * _silu(z_ref[rows, :].astype(F32))).astype(BF16)

    front = (n_trips - 1) * chunks_per_trip * CHUNK
    back = seq - front
    mid_rows = (front - back) // chunks_per_trip

    def drain_step(k, carry):
        finish_rows(pl.multiple_of(back + k * mid_rows, mid_rows), mid_rows)
        prev = jnp.maximum(k - 1, 0)
        finish_rows(pl.multiple_of(front + prev * CHUNK, CHUNK), CHUNK)
        finish_rows(pl.multiple_of(back - (prev + 1) * CHUNK, CHUNK), CHUNK)
        drain_states = [s_scr[0], s_scr[1]]
        recurrence_step(drain_states, k, n_trips - 1)
        s_scr[0] = drain_states[0]
        s_scr[1] = drain_states[1]
        return carry

    lax.fori_loop(0, chunks_per_trip, drain_step, 0)
    finish_rows(seq - CHUNK, CHUNK)
    finish_rows(0, CHUNK)


def _deltanet(proj, gate_rows, conv_w3, a_log, dt_bias, head_norm_w, *, batch, seq, n_heads, qkv_block0, z_block0):
    n_steps = batch * n_heads
    chunks_per_trip = GROUP_UNROLL * CHUNKS_PER_GROUP
    smem = pl.BlockSpec(memory_space=pltpu.SMEM)

    def next_step(b, h):
        lin = jnp.minimum(b * n_heads + h + 1, n_steps - 1)
        return lin // n_heads, lin % n_heads

    def first_cols(off):
        return pl.BlockSpec((seq, HEAD_DIM), lambda b, h: (0, off), pipeline_mode=pl.Buffered(1))

    def next_cols(off):
        def index(b, h):
            nb, nh = next_step(b, h)
            return nb, off + nh
        return pl.BlockSpec((seq, HEAD_DIM), index)

    def next_gates(b, h):
        nb, nh = next_step(b, h)
        return nh, 0, nb

    def next_conv(off):
        return pl.BlockSpec((None, 8, HEAD_DIM), lambda b, h: (off + next_step(b, h)[1], 0, 0))

    first_conv = lambda off: pl.BlockSpec((None, 8, HEAD_DIM), lambda b, h: (off, 0, 0))
    k_off, v_off = qkv_block0 + n_heads, qkv_block0 + 2 * n_heads
    return pl.pallas_call(
        functools.partial(_dn_body, n_heads=n_heads, n_steps=n_steps),
        grid=(batch, n_heads),
        in_specs=[
            smem, smem,
            first_cols(qkv_block0), first_cols(k_off), first_cols(v_off),
            pl.BlockSpec((None, GATE_ROWS, seq), lambda b, h: (0, 0, 0)),
            first_conv(0), first_conv(n_heads), first_conv(2 * n_heads),
            next_cols(qkv_block0), next_cols(k_off), next_cols(v_off),
            pl.BlockSpec((None, GATE_ROWS, seq), next_gates),
            next_conv(0), next_conv(n_heads), next_conv(2 * n_heads),
            pl.BlockSpec((seq, HEAD_DIM), lambda b, h: (b, z_block0 + h)),
            pl.BlockSpec((1, HEAD_DIM), lambda b, h: (0, 0)),
        ],
        out_specs=pl.BlockSpec((seq, HEAD_DIM), lambda b, h: (b, h)),
        out_shape=jax.ShapeDtypeStruct((batch * seq, n_heads * HEAD_DIM), BF16),
        scratch_shapes=[
            pltpu.VMEM((seq + 2 * CONV_PAD, HEAD_DIM), F32),
            pltpu.VMEM((2, seq, HEAD_DIM), BF16),
            pltpu.VMEM((2, seq, HEAD_DIM), BF16),
            pltpu.VMEM((2, seq, HEAD_DIM), BF16),
            pltpu.VMEM((2, HEAD_DIM, seq), BF16),
            pltpu.VMEM((2, GATE_ROWS, seq), F32),
            pltpu.VMEM((2, GATE_ROWS, seq), F32),
            pltpu.VMEM((2, GATE_ROWS, seq), F32),
            pltpu.VMEM((2, 2, seq, LANES), F32),
            pltpu.VMEM((seq // GROUP, 2, CHUNK, GROUP), F32),
            pltpu.VMEM((2, 2, chunks_per_trip, HEAD_DIM + CHUNK, HEAD_DIM), BF16),
            pltpu.VMEM((2, 2, chunks_per_trip, HEAD_DIM, HEAD_DIM), F32),
            pltpu.VMEM((2, HEAD_DIM, HEAD_DIM), F32),
            pltpu.VMEM((seq, HEAD_DIM), F32),
        ],
        compiler_params=pltpu.CompilerParams(
            dimension_semantics=("arbitrary", "arbitrary"), vmem_limit_bytes=62 * MIB),
        name="deltanet",
    )(a_log, dt_bias,
      proj, proj, proj, gate_rows, conv_w3, conv_w3, conv_w3,
      proj, proj, proj, gate_rows, conv_w3, conv_w3, conv_w3,
      proj, head_norm_w)


def _outproj_body(yp_ref, yd_ref, x_ref, w1_ref, w2_ref, fnw_ref, o_ref, *, final):
    acc = x_ref[...] + _dot(yp_ref[...], w1_ref[...].astype(BF16)) + _dot(yd_ref[...], w2_ref[...].astype(BF16))
    if final:
        acc = acc * lax.rsqrt(jnp.mean(acc * acc, axis=-1, keepdims=True) + EPS) * fnw_ref[...]
    o_ref[...] = acc


def _outproj(y_pool, y_dn, x2, w_out, layer, final_norm_w, *, final, row_tile=1024):
    m, d = x2.shape
    k1, k2 = y_pool.shape[1], y_dn.shape[1]
    assert k1 == k2
    return pl.pallas_call(
        functools.partial(_outproj_body, final=final),
        grid=(m // row_tile,),
        in_specs=[
            pl.BlockSpec((row_tile, k1), lambda i: (i, 0)),
            pl.BlockSpec((row_tile, k2), lambda i: (i, 0)),
            pl.BlockSpec((row_tile, d), lambda i: (i, 0)),
            pl.BlockSpec((None, k1, d), lambda i: (layer, 0, 0), pipeline_mode=pl.Buffered(1)),
            pl.BlockSpec((None, k2, d), lambda i: (layer, 1, 0), pipeline_mode=pl.Buffered(1)),
            pl.BlockSpec((1, d), lambda i: (0, 0)),
        ],
        out_specs=pl.BlockSpec((row_tile, d), lambda i: (i, 0)),
        out_shape=jax.ShapeDtypeStruct((m, d), F32),
        compiler_params=pltpu.CompilerParams(
            dimension_semantics=("arbitrary",), vmem_limit_bytes=40 * MIB),
        name="outproj",
    )(y_pool, y_dn, x2, w_out, w_out, final_norm_w)


def kernel(x, norm_w, w_in, pool_w, pool_scale, conv_w, a_log, dt_bias, head_norm_w, w_out, final_norm_w):
    batch, seq, d_model = x.shape
    depth = norm_w.shape[0]
    pool_g = pool_w.shape[-1]
    pool_width = N_POOL_GROUPS * pool_g
    n_dirs, n_heads = a_log.shape[1], a_log.shape[2]
    dn_width = n_heads * HEAD_DIM
    n_main = 2 * pool_width + 4 * dn_width
    n_gate = 2 * n_dirs * n_heads
    qkv_block0 = 2 * pool_width // HEAD_DIM
    z_block0 = qkv_block0 + 3 * n_heads

    x2 = x.reshape(batch * seq, d_model)
    w_in_b = w_in.astype(BF16)
    for l in range(depth):
        w_gate = w_in[l, :, n_main:n_main + n_gate].reshape(d_model, 2, n_dirs, n_heads)[:, ::-1]
        w_gate = jnp.transpose(w_gate, (0, 3, 1, 2)).reshape(d_model, n_heads, 2 * n_dirs)
        w_gate = jnp.pad(w_gate, ((0, 0), (0, 0), (0, GATE_ROWS - 2 * n_dirs))).reshape(d_model, n_heads * GATE_ROWS)
        proj, gate_t = _inproj(x2, norm_w[l][None, :], w_in_b, l, w_gate.T.astype(BF16),
                               n_main=n_main, gate_rows=n_heads * GATE_ROWS)
        gate_rows = gate_t.reshape(n_heads, GATE_ROWS, batch * seq)
        conv_w3 = jnp.pad(conv_w[l], ((0, 8 - CONV_K), (0, 0))).reshape(8, 3 * n_heads, HEAD_DIM)
        conv_w3 = jnp.transpose(conv_w3, (1, 0, 2))
        y_pool = _pool(proj, pool_w[l].astype(BF16), pool_scale[l].reshape(N_POOL_GROUPS, 1, pool_g),
                       batch=batch, seq=seq)
        y_dn = _deltanet(proj, gate_rows, conv_w3, a_log[l], dt_bias[l], head_norm_w[l][None, :],
                         batch=batch, seq=seq, n_heads=n_heads, qkv_block0=qkv_block0, z_block0=z_block0)
        x2 = _outproj(y_pool, y_dn, x2, w_out, l, final_norm_w[None, :], final=(l == depth - 1))
    return x2.reshape(batch, seq, d_model)
```

```python
import functools

import jax
import jax.numpy as jnp
from jax import lax
from jax.experimental import pallas as pl
from jax.experimental.pallas import tpu as pltpu

F32 = jnp.float32
BF16 = jnp.bfloat16

EPS = 1e-6
POOL_WINDOWS = (2, 4, 8, 16)
N_POOL_GROUPS = len(POOL_WINDOWS)
HEAD_DIM = 128
CONV_K = 5
CHUNK = 64
GROUP = 256
CHUNKS_PER_GROUP = GROUP // CHUNK
GROUP_UNROLL = 4
REC_EVERY = 6
PREP_EVERY = 10
GATE_ROWS = 8
LANES = 128
MIB = 1024 * 1024


def _sigmoid(x):
    return 0.5 * jnp.tanh(0.5 * x) + 0.5


def _silu(x):
    half = 0.5 * x
    return half + half * jnp.tanh(half)


def _dot(a, b):
    return jnp.dot(a, b, preferred_element_type=F32)


def _inproj_body(x_ref, nw_ref, w_ref, wg_ref, o_ref, g_ref, h_scr, *, col_tile):
    x = x_ref[...]
    ms = jnp.mean(x * x, axis=-1, keepdims=True)
    h_scr[...] = (x * lax.rsqrt(ms + EPS) * nw_ref[...]).astype(BF16)
    g_ref[...] = lax.dot_general(wg_ref[...], h_scr[...], (((1,), (1,)), ((), ())), preferred_element_type=F32)
    for n in range(w_ref.shape[1] // col_tile):
        cols = slice(n * col_tile, (n + 1) * col_tile)
        o_ref[:, cols] = _dot(h_scr[...], w_ref[:, cols]).astype(BF16)


def _inproj(x2, norm_w, w_in_b, layer, w_gate, *, n_main, gate_rows, row_tile=1024, col_tile=512):
    m, d = x2.shape
    return pl.pallas_call(
        functools.partial(_inproj_body, col_tile=col_tile),
        grid=(m // row_tile,),
        in_specs=[
            pl.BlockSpec((row_tile, d), lambda i: (i, 0)),
            pl.BlockSpec((1, d), lambda i: (0, 0)),
            pl.BlockSpec((None, d, n_main), lambda i: (layer, 0, 0), pipeline_mode=pl.Buffered(1)),
            pl.BlockSpec((gate_rows, d), lambda i: (0, 0)),
        ],
        out_specs=[
            pl.BlockSpec((row_tile, n_main), lambda i: (i, 0)),
            pl.BlockSpec((gate_rows, row_tile), lambda i: (0, i)),
        ],
        out_shape=[
            jax.ShapeDtypeStruct((m, n_main), BF16),
            jax.ShapeDtypeStruct((gate_rows, m), F32),
        ],
        scratch_shapes=[pltpu.VMEM((row_tile, d), BF16)],
        compiler_params=pltpu.CompilerParams(
            dimension_semantics=("arbitrary",), vmem_limit_bytes=56 * MIB),
        name="inproj",
    )(x2, norm_w, w_in_b, w_gate)


POOL_TILE = 256
POOL_HALO = 16


def _pool_body(u_ref, z_ref, pw_ref, ps_ref, o_ref):
    seq = u_ref.shape[0]
    g = pl.program_id(1)
    left = jnp.left_shift(jnp.int32(1), g)
    right = left - 1
    i = lax.broadcasted_iota(jnp.int32, (POOL_TILE, POOL_TILE), 0)
    j = lax.broadcasted_iota(jnp.int32, (POOL_TILE, POOL_TILE), 1)
    band = ((j >= i - left) & (j <= i + right)).astype(BF16)
    ih = lax.broadcasted_iota(jnp.int32, (POOL_TILE, 2 * POOL_HALO), 0)
    ph = lax.broadcasted_iota(jnp.int32, (POOL_TILE, 2 * POOL_HALO), 1)
    in_prev = (ph < POOL_HALO) & (ph - POOL_HALO >= ih - left)
    in_next = (ph >= POOL_HALO) & (ph - POOL_HALO + POOL_TILE <= ih + right)
    halo_band = (in_prev | in_next).astype(BF16)
    pw = pw_ref[...]
    ps = ps_ref[...]
    n_tiles = seq // POOL_TILE
    tok0 = lax.broadcasted_iota(jnp.int32, (POOL_TILE, u_ref.shape[1]), 0)

    def inv_count(t0):
        tok = tok0 + t0
        cnt = jnp.minimum(tok + right, seq - 1) - jnp.maximum(tok - left, 0) + 1
        return 1.0 / cnt.astype(F32)

    inv_first, inv_mid, inv_last = inv_count(0), inv_count(POOL_TILE), inv_count(seq - POOL_TILE)
    zero_halo = jnp.zeros((POOL_HALO, u_ref.shape[1]), BF16)

    def window_sums(t):
        t0 = t * POOL_TILE
        prev = u_ref[t0 - POOL_HALO:t0, :] if t > 0 else zero_halo
        nxt = u_ref[t0 + POOL_TILE:t0 + POOL_TILE + POOL_HALO, :] if t < n_tiles - 1 else zero_halo
        return _dot(band, u_ref[t0:t0 + POOL_TILE, :]) + _dot(halo_band, jnp.concatenate([prev, nxt], axis=0))

    wsum = window_sums(0)
    for t in range(n_tiles):
        t0 = t * POOL_TILE
        wsum_next = window_sums(t + 1) if t < n_tiles - 1 else None
        inv = inv_first if t == 0 else (inv_last if t == n_tiles - 1 else inv_mid)
        m = wsum * inv - u_ref[t0:t0 + POOL_TILE, :].astype(F32)
        y = _dot(m.astype(BF16), pw) * ps
        o_ref[t0:t0 + POOL_TILE, :] = (y * _silu(z_ref[t0:t0 + POOL_TILE, :].astype(F32))).astype(BF16)
        wsum = wsum_next


def _pool(proj, pool_w, pool_scale, *, batch, seq):
    pool_g = pool_w.shape[-1]
    return pl.pallas_call(
        _pool_body,
        grid=(batch, N_POOL_GROUPS),
        in_specs=[
            pl.BlockSpec((seq, pool_g), lambda b, g: (b, g)),
            pl.BlockSpec((seq, pool_g), lambda b, g: (b, N_POOL_GROUPS + g)),
            pl.BlockSpec((None, pool_g, pool_g), lambda b, g: (g, 0, 0)),
            pl.BlockSpec((None, 1, pool_g), lambda b, g: (g, 0, 0)),
        ],
        out_specs=pl.BlockSpec((seq, pool_g), lambda b, g: (b, g)),
        out_shape=jax.ShapeDtypeStruct((batch * seq, N_POOL_GROUPS * pool_g), BF16),
        compiler_params=pltpu.CompilerParams(
            dimension_semantics=("arbitrary", "arbitrary"), vmem_limit_bytes=40 * MIB),
        name="pool",
    )(proj, proj, pool_w, pool_scale)


CONV_TILE = 512
CONV_PAD = 16
REP_TILES_PER_TASK = 8


def _seg_scan(x, pos, *, suffix):
    n = x.shape[1]
    s = 1
    while s < CHUNK:
        if suffix:
            x = x + jnp.where(pos < CHUNK - s, pltpu.roll(x, n - s, axis=1), 0.0)
        else:
            x = x + jnp.where(pos >= s, pltpu.roll(x, s, axis=1), 0.0)
        s *= 2
    return x


def _dn_body(alog_ref, dtb_ref,
             q0_ref, k0_ref, v0_ref, gl0_ref, cwq0_ref, cwk0_ref, cwv0_ref,
             qn_ref, kn_ref, vn_ref, gln_ref, cwqn_ref, cwkn_ref, cwvn_ref,
             z_ref, hnw_ref,
             o_ref,
             xpad, qh, kh, vh, kt, gc_row, er_row, beta_row, rep, aq_scr, b_scr, s_scr, o_acc,
             *, n_heads, n_steps):
    seq = z_ref.shape[0]
    n_chunks = seq // CHUNK
    n_groups = seq // GROUP
    n_trips = n_groups // GROUP_UNROLL
    chunks_per_trip = GROUP_UNROLL * CHUNKS_PER_GROUP
    step = pl.program_id(0) * n_heads + pl.program_id(1)
    slot = lax.rem(step, 2)
    next_head = lax.rem(jnp.minimum(step + 1, n_steps - 1), n_heads)

    def prep_tasks(q_ref, k_ref, v_ref, gl_ref, cw_refs, head, w):
        tasks = []

        def zero_pads():
            xpad[0:CONV_PAD, :] = jnp.zeros((CONV_PAD, HEAD_DIM), F32)
            xpad[CONV_PAD + seq:CONV_PAD + seq + CONV_PAD, :] = jnp.zeros((CONV_PAD, HEAD_DIM), F32)

        tasks.append(zero_pads)
        for which, (src, cw_ref) in enumerate(zip((q_ref, k_ref, v_ref), cw_refs)):
            def fill(src=src):
                for t in range(seq // CONV_TILE):
                    t0 = t * CONV_TILE
                    xpad[CONV_PAD + t0:CONV_PAD + t0 + CONV_TILE, :] = src[t0:t0 + CONV_TILE, :].astype(F32)

            tasks.append(fill)
            for t in range(seq // CONV_TILE):
                def conv(which=which, cw_ref=cw_ref, t0=t * CONV_TILE):
                    acc = jnp.zeros((CONV_TILE, HEAD_DIM), F32)
                    for tap in range(CONV_K):
                        lo = CONV_PAD + t0 + tap - CONV_K // 2
                        acc = acc + cw_ref[tap:tap + 1, :] * xpad[lo:lo + CONV_TILE, :]
                    s = _silu(acc)
                    rows = slice(t0, t0 + CONV_TILE)
                    if which == 0:
                        s = s * (lax.rsqrt(jnp.sum(s * s, axis=-1, keepdims=True) + EPS) * (HEAD_DIM ** -0.5))
                        qh[w, rows, :] = s.astype(BF16)
                    elif which == 1:
                        s = s * lax.rsqrt(jnp.sum(s * s, axis=-1, keepdims=True) + EPS)
                        kh[w, rows, :] = s.astype(BF16)
                        kt[w, :, rows] = s.T.astype(BF16)
                    else:
                        vh[w, rows, :] = s.astype(BF16)

                tasks.append(conv)

        def gates():
            r = gl_ref[...]
            rowi = lax.broadcasted_iota(jnp.int32, r.shape, 0)
            pos = lax.broadcasted_iota(jnp.int32, r.shape, 1) & (CHUNK - 1)
            fwd_row = rowi == 0
            dtb = jnp.where(fwd_row, dtb_ref[0, head], dtb_ref[1, head])
            a_coef = jnp.exp(jnp.where(fwd_row, alog_ref[0, head], alog_ref[1, head]))
            g = jnp.where(rowi < 2, -a_coef * jax.nn.softplus(r + dtb), 0.0)
            pre = _seg_scan(g, pos, suffix=False)
            suf = _seg_scan(g, pos, suffix=True)
            gc = jnp.where(fwd_row, pre, suf)
            tot = pre + suf - g
            gc_row[w] = gc
            er_row[w] = jnp.exp(tot - gc)
            beta_row[w] = _sigmoid(r)

        tasks.append(gates)
        for d in range(2):
            for t8 in range(0, seq // LANES, REP_TILES_PER_TASK):
                def replicate(d=d, t8=t8):
                    for t in range(t8, t8 + REP_TILES_PER_TASK):
                        cols = slice(t * LANES, (t + 1) * LANES)
                        tile = jnp.broadcast_to(gc_row[w, d:d + 1, cols], (LANES, LANES))
                        rep[w, d, cols, :] = tile.T

                tasks.append(replicate)
        return tasks

    @pl.when(step == 0)
    def _():
        for task in prep_tasks(q0_ref, k0_ref, v0_ref, gl0_ref, (cwq0_ref, cwk0_ref, cwv0_ref), 0, 0):
            task()

    next_tasks = prep_tasks(qn_ref, kn_ref, vn_ref, gln_ref, (cwqn_ref, cwkn_ref, cwvn_ref), next_head, 1 - slot)

    ci = lax.broadcasted_iota(jnp.int32, (CHUNK, GROUP), 0)
    cl = lax.broadcasted_iota(jnp.int32, (CHUNK, GROUP), 1)
    cj = cl & (CHUNK - 1)
    lane_chunk = cl // CHUNK
    eye_c = (cj == ci).astype(F32)
    first_half = lax.broadcasted_iota(jnp.int32, (CHUNK, LANES), 1) < CHUNK
    bi = lax.broadcasted_iota(jnp.int32, (GROUP, GROUP), 0)
    bj = lax.broadcasted_iota(jnp.int32, (GROUP, GROUP), 1)
    bd_mask = ((bi // CHUNK) == (bj // CHUNK)).astype(F32).astype(BF16)

    def compact(full):
        out = full[(CHUNKS_PER_GROUP - 1) * CHUNK:, :]
        for c in range(CHUNKS_PER_GROUP - 2, -1, -1):
            out = jnp.where(lane_chunk == c, full[c * CHUNK:(c + 1) * CHUNK, :], out)
        return out

    def per_chunk_lanes(tok):
        blk = [tok[c * CHUNK:(c + 1) * CHUNK, :] for c in range(CHUNKS_PER_GROUP)]
        return jnp.concatenate([jnp.where(first_half, blk[0], blk[1]), jnp.where(first_half, blk[2], blk[3])], axis=1)

    def block_diag(cmp):
        cb = cmp.astype(BF16)
        return jnp.concatenate([cb] * CHUNKS_PER_GROUP, axis=0) * bd_mask

    def chunk_lanes(x, c):
        half = x[:, (c // 2) * LANES:(c // 2 + 1) * LANES]
        if c % 2:
            half = pltpu.roll(half, CHUNK, axis=1)
        return half[:, :CHUNK]

    o_acc[...] = jnp.zeros(o_acc.shape, F32)

    def recurrence_step(states, k, trip):
        for d in range(2):
            lin = trip * chunks_per_trip + k
            c = lin if d == 0 else n_chunks - 1 - lin
            r0 = c * CHUNK
            if not isinstance(r0, int):
                r0 = pl.multiple_of(r0, CHUNK)
            res = _dot(aq_scr[trip % 2, d, k], states[d].astype(BF16))
            last = r0 + (CHUNK - 1 if d == 0 else 0)
            dec = jnp.exp(rep[slot, d, pl.ds(last, 1), :])
            states[d] = states[d] * dec - res[:HEAD_DIM] + b_scr[trip % 2, d, k]
            o_acc[pl.ds(r0, CHUNK), :] += res[HEAD_DIM:]

    def precompute(trip, tick):
        chains = []
        for d in range(2):
            incl = (cj <= ci) if d == 0 else (cj >= ci)
            strict = (cj < ci) if d == 0 else (cj > ci)
            for k in range(GROUP_UNROLL):
                lin = trip * GROUP_UNROLL + k
                gi = lin if d == 0 else n_groups - 1 - lin
                rows = slice(gi * GROUP, (gi + 1) * GROUP)
                kt_g = kt[slot, :, rows]
                kk = compact(_dot(kh[slot, rows, :], kt_g))
                tick()
                qk = compact(_dot(qh[slot, rows, :], kt_g))
                tick()
                diff = per_chunk_lanes(rep[slot, d, rows, :]) - gc_row[slot, d:d + 1, rows]
                decay = jnp.where(incl, jnp.exp(jnp.where(incl, diff, 0.0)), 0.0)
                decay_beta = decay * beta_row[slot, 2 + d:3 + d, rows]
                mk = jnp.where(strict, -(kk * decay_beta), 0.0)
                a = jnp.where(incl, qk * decay_beta, 0.0)
                chains.append(dict(k=k, r0=gi * GROUP, rows=rows, d=d, mk=mk, tinv=eye_c + mk, a=a))
        for ch in chains:
            ch["mk"] = _dot(ch["mk"].astype(BF16), block_diag(ch["mk"]))
            tick()
        p = 2
        while p < CHUNK // 2:
            for ch in chains:
                x2 = _dot(jnp.concatenate([ch["tinv"], ch["mk"]], axis=0).astype(BF16), block_diag(ch["mk"]))
                tick()
                ch["tinv"] = ch["tinv"] + x2[:CHUNK]
                ch["mk"] = x2[CHUNK:]
            p *= 2
        for ch in chains:
            ch["tinv"] = ch["tinv"] + _dot(ch["tinv"].astype(BF16), block_diag(ch["mk"]))
            tick()
        for ch in chains:
            d, rows = ch["d"], ch["rows"]
            ke = (kh[slot, rows, :].astype(F32) * jnp.exp(rep[slot, d, rows, :])).astype(BF16)
            rhs = jnp.concatenate([ke, vh[slot, rows, :]], axis=1)
            ch["wu"] = _dot(block_diag(ch["tinv"]), rhs).astype(BF16)
            tick()
        for ch in chains:
            d, rows = ch["d"], ch["rows"]
            q_til = qh[slot, rows, :].astype(F32) * jnp.exp(rep[slot, d, rows, :])
            kt_til = kt[slot, :, rows].astype(F32) * (er_row[slot, d:d + 1, rows] * beta_row[slot, 2 + d:3 + d, rows])
            lhs_all = jnp.concatenate([kt_til, ch["a"]], axis=0)
            for c in range(CHUNKS_PER_GROUP):
                crow = slice(c * CHUNK, (c + 1) * CHUNK)
                res = _dot(chunk_lanes(lhs_all, c).astype(BF16), ch["wu"][crow, :])
                tick()
                loc = ch["k"] * CHUNKS_PER_GROUP + (c if d == 0 else CHUNKS_PER_GROUP - 1 - c)
                aq_scr[trip % 2, d, loc, 0:HEAD_DIM, :] = res[:HEAD_DIM, :HEAD_DIM].astype(BF16)
                aq_scr[trip % 2, d, loc, HEAD_DIM:, :] = (q_til[crow, :] - res[HEAD_DIM:, :HEAD_DIM]).astype(BF16)
                b_scr[trip % 2, d, loc] = res[:HEAD_DIM, HEAD_DIM:]
                o_acc[ch["r0"] + c * CHUNK:ch["r0"] + (c + 1) * CHUNK, :] += res[HEAD_DIM:, HEAD_DIM:]

    states = [jnp.zeros((HEAD_DIM, HEAD_DIM), F32), jnp.zeros((HEAD_DIM, HEAD_DIM), F32)]
    progress = dict(dots=0, prep=0)
    for trip in range(n_trips):
        progress["steps"] = 0 if trip > 0 else chunks_per_trip

        def tick(trip=trip):
            progress["dots"] += 1
            if progress["dots"] % REC_EVERY == 0 and progress["steps"] < chunks_per_trip:
                recurrence_step(states, progress["steps"], trip - 1)
                progress["steps"] += 1
            if progress["dots"] % PREP_EVERY == 0 and progress["prep"] < len(next_tasks):
                next_tasks[progress["prep"]]()
                progress["prep"] += 1

        precompute(trip, tick)
        while progress["steps"] < chunks_per_trip:
            recurrence_step(states, progress["steps"], trip - 1)
            progress["steps"] += 1
    for task in next_tasks[progress["prep"]:]:
        task()
    s_scr[0] = states[0]
    s_scr[1] = states[1]

    hnw = hnw_ref[...]

    def finish_rows(r0, n):
        rows = pl.ds(r0, n)
        o = o_acc[rows, :]
        y = o * lax.rsqrt(jnp.mean(o * o, axis=-1, keepdims=True) + EPS) * hnw
        o_ref[rows, :] = (y * _silu(z_ref[rows, :].astype(F32))).astype(BF16)

    front = (n_trips - 1) * chunks_per_trip * CHUNK
    back = seq - front
    mid_rows = (front - back) // chunks_per_trip

    def drain_step(k, carry):
        finish_rows(pl.multiple_of(back + k * mid_rows, mid_rows), mid_rows)
        prev = jnp.maximum(k - 1, 0)
        finish_rows(pl.multiple_of(front + prev * CHUNK, CHUNK), CHUNK)
        finish_rows(pl.multiple_of(back - (prev + 1) * CHUNK, CHUNK), CHUNK)
        drain_states = [s_scr[0], s_scr[1]]
        recurrence_step(drain_states, k, n_trips - 1)
        s_scr[0] = drain_states[0]
        s_scr[1] = drain_states[1]
        return carry

    lax.fori_loop(0, chunks_per_trip, drain_step, 0)
    finish_rows(seq - CHUNK, CHUNK)
    finish_rows(0, CHUNK)


def _deltanet(proj, gate_rows, conv_w3, a_log, dt_bias, head_norm_w, *, batch, seq, n_heads, qkv_block0, z_block0):
    n_steps = batch * n_heads
    chunks_per_trip = GROUP_UNROLL * CHUNKS_PER_GROUP
    smem = pl.BlockSpec(memory_space=pltpu.SMEM)

    def next_step(b, h):
        lin = jnp.minimum(b * n_heads + h + 1, n_steps - 1)
        return lin // n_heads, lin % n_heads

    def first_cols(off):
        return pl.BlockSpec((seq, HEAD_DIM), lambda b, h: (0, off), pipeline_mode=pl.Buffered(1))

    def next_cols(off):
        def index(b, h):
            nb, nh = next_step(b, h)
            return nb, off + nh
        return pl.BlockSpec((seq, HEAD_DIM), index)

    def next_gates(b, h):
        nb, nh = next_step(b, h)
        return nh, 0, nb

    def next_conv(off):
        return pl.BlockSpec((None, 8, HEAD_DIM), lambda b, h: (off + next_step(b, h)[1], 0, 0))

    first_conv = lambda off: pl.BlockSpec((None, 8, HEAD_DIM), lambda b, h: (off, 0, 0))
    k_off, v_off = qkv_block0 + n_heads, qkv_block0 + 2 * n_heads
    return pl.pallas_call(
        functools.partial(_dn_body, n_heads=n_heads, n_steps=n_steps),
        grid=(batch, n_heads),
        in_specs=[
            smem, smem,
            first_cols(qkv_block0), first_cols(k_off), first_cols(v_off),
            pl.BlockSpec((None, GATE_ROWS, seq), lambda b, h: (0, 0, 0)),
            first_conv(0), first_conv(n_heads), first_conv(2 * n_heads),
            next_cols(qkv_block0), next_cols(k_off), next_cols(v_off),
            pl.BlockSpec((None, GATE_ROWS, seq), next_gates),
            next_conv(0), next_conv(n_heads), next_conv(2 * n_heads),
            pl.BlockSpec((seq, HEAD_DIM), lambda b, h: (b, z_block0 + h)),
            pl.BlockSpec((1, HEAD_DIM), lambda b, h: (0, 0)),
        ],
        out_specs=pl.BlockSpec((seq, HEAD_DIM), lambda b, h: (b, h)),
        out_shape=jax.ShapeDtypeStruct((batch * seq, n_heads * HEAD_DIM), BF16),
        scratch_shapes=[
            pltpu.VMEM((seq + 2 * CONV_PAD, HEAD_DIM), F32),
            pltpu.VMEM((2, seq, HEAD_DIM), BF16),
            pltpu.VMEM((2, seq, HEAD_DIM), BF16),
            pltpu.VMEM((2, seq, HEAD_DIM), BF16),
            pltpu.VMEM((2, HEAD_DIM, seq), BF16),
            pltpu.VMEM((2, GATE_ROWS, seq), F32),
            pltpu.VMEM((2, GATE_ROWS, seq), F32),
            pltpu.VMEM((2, GATE_ROWS, seq), F32),
            pltpu.VMEM((2, 2, seq, LANES), F32),
            pltpu.VMEM((2, 2, chunks_per_trip, HEAD_DIM + CHUNK, HEAD_DIM), BF16),
            pltpu.VMEM((2, 2, chunks_per_trip, HEAD_DIM, HEAD_DIM), F32),
            pltpu.VMEM((2, HEAD_DIM, HEAD_DIM), F32),
            pltpu.VMEM((seq, HEAD_DIM), F32),
        ],
        compiler_params=pltpu.CompilerParams(
            dimension_semantics=("arbitrary", "arbitrary"), vmem_limit_bytes=62 * MIB),
        name="deltanet",
    )(a_log, dt_bias,
      proj, proj, proj, gate_rows, conv_w3, conv_w3, conv_w3,
      proj, proj, proj, gate_rows, conv_w3, conv_w3, conv_w3,
      proj, head_norm_w)


def _outproj_body(yp_ref, yd_ref, x_ref, w1_ref, w2_ref, fnw_ref, o_ref, *, final):
    acc = x_ref[...] + _dot(yp_ref[...], w1_ref[...].astype(BF16)) + _dot(yd_ref[...], w2_ref[...].astype(BF16))
    if final:
        acc = acc * lax.rsqrt(jnp.mean(acc * acc, axis=-1, keepdims=True) + EPS) * fnw_ref[...]
    o_ref[...] = acc


def _outproj(y_pool, y_dn, x2, w_out, layer, final_norm_w, *, final, row_tile=1024):
    m, d = x2.shape
    k1, k2 = y_pool.shape[1], y_dn.shape[1]
    assert k1 == k2
    return pl.pallas_call(
        functools.partial(_outproj_body, final=final),
        grid=(m // row_tile,),
        in_specs=[
            pl.BlockSpec((row_tile, k1), lambda i: (i, 0)),
            pl.BlockSpec((row_tile, k2), lambda i: (i, 0)),
            pl.BlockSpec((row_tile, d), lambda i: (i, 0)),
            pl.BlockSpec((None, k1, d), lambda i: (layer, 0, 0), pipeline_mode=pl.Buffered(1)),
            pl.BlockSpec((None, k2, d), lambda i: (layer, 1, 0), pipeline_mode=pl.Buffered(1)),
            pl.BlockSpec((1, d), lambda i: (0, 0)),
        ],
        out_specs=pl.BlockSpec((row_tile, d), lambda i: (i, 0)),
        out_shape=jax.ShapeDtypeStruct((m, d), F32),
        compiler_params=pltpu.CompilerParams(
            dimension_semantics=("arbitrary",), vmem_limit_bytes=40 * MIB),
        name="outproj",
    )(y_pool, y_dn, x2, w_out, w_out, final_norm_w)


def kernel(x, norm_w, w_in, pool_w, pool_scale, conv_w, a_log, dt_bias, head_norm_w, w_out, final_norm_w):
    batch, seq, d_model = x.shape
    depth = norm_w.shape[0]
    pool_g = pool_w.shape[-1]
    pool_width = N_POOL_GROUPS * pool_g
    n_dirs, n_heads = a_log.shape[1], a_log.shape[2]
    dn_width = n_heads * HEAD_DIM
    n_main = 2 * pool_width + 4 * dn_width
    n_gate = 2 * n_dirs * n_heads
    qkv_block0 = 2 * pool_width // HEAD_DIM
    z_block0 = qkv_block0 + 3 * n_heads

    x2 = x.reshape(batch * seq, d_model)
    w_in_b = w_in.astype(BF16)
    for l in range(depth):
        w_gate = w_in[l, :, n_main:n_main + n_gate].reshape(d_model, 2, n_dirs, n_heads)[:, ::-1]
        w_gate = jnp.transpose(w_gate, (0, 3, 1, 2)).reshape(d_model, n_heads, 2 * n_dirs)
        w_gate = jnp.pad(w_gate, ((0, 0), (0, 0), (0, GATE_ROWS - 2 * n_dirs))).reshape(d_model, n_heads * GATE_ROWS)
        proj, gate_t = _inproj(x2, norm_w[l][None, :], w_in_b, l, w_gate.T.astype(BF16),
                               n_main=n_main, gate_rows=n_heads * GATE_ROWS)
        gate_rows = gate_t.reshape(n_heads, GATE_ROWS, batch * seq)
        conv_w3 = jnp.pad(conv_w[l], ((0, 8 - CONV_K), (0, 0))).reshape(8, 3 * n_heads, HEAD_DIM)
        conv_w3 = jnp.transpose(conv_w3, (1, 0, 2))
        y_pool = _pool(proj, pool_w[l].astype(BF16), pool_scale[l].reshape(N_POOL_GROUPS, 1, pool_g),
                       batch=batch, seq=seq)
        y_dn = _deltanet(proj, gate_rows, conv_w3, a_log[l], dt_bias[l], head_norm_w[l][None, :],
                         batch=batch, seq=seq, n_heads=n_heads, qkv_block0=qkv_block0, z_block0=z_block0)
        x2 = _outproj(y_pool, y_dn, x2, w_out, l, final_norm_w[None, :], final=(l == depth - 1))
    return x2.reshape(batch, seq, d_model)
```

```python
import functools

import jax
import jax.numpy as jnp
from jax import lax
from jax.experimental import pallas as pl
from jax.experimental.pallas import tpu as pltpu

F32 = jnp.float32
BF16 = jnp.bfloat16

EPS = 1e-6
POOL_WINDOWS = (2, 4, 8, 16)
N_POOL_GROUPS = len(POOL_WINDOWS)
HEAD_DIM = 128
CONV_K = 5
CHUNK = 64
GROUP = 256
CHUNKS_PER_GROUP = GROUP // CHUNK
GROUP_UNROLL = 4
REC_EVERY = 6
PREP_EVERY = 10
GATE_ROWS = 8
LANES = 128
MIB = 1024 * 1024


def _sigmoid(x):
    return 0.5 * jnp.tanh(0.5 * x) + 0.5


def _silu(x):
    half = 0.5 * x
    return half + half * jnp.tanh(half)


def _dot(a, b):
    return jnp.dot(a, b, preferred_element_type=F32)


def _inproj_body(x_ref, nw_ref, w_ref, wg_ref, o_ref, g_ref, h_scr, *, col_tile):
    x = x_ref[...]
    ms = jnp.mean(x * x, axis=-1, keepdims=True)
    h_scr[...] = (x * lax.rsqrt(ms + EPS) * nw_ref[...]).astype(BF16)
    g_ref[...] = lax.dot_general(wg_ref[...], h_scr[...], (((1,), (1,)), ((), ())), preferred_element_type=F32)
    for n in range(w_ref.shape[1] // col_tile):
        cols = slice(n * col_tile, (n + 1) * col_tile)
        o_ref[:, cols] = _dot(h_scr[...], w_ref[:, cols]).astype(BF16)


def _inproj(x2, norm_w, w_in_b, layer, w_gate, *, n_main, gate_rows, row_tile=1024, col_tile=512):
    m, d = x2.shape
    return pl.pallas_call(
        functools.partial(_inproj_body, col_tile=col_tile),
        grid=(m // row_tile,),
        in_specs=[
            pl.BlockSpec((row_tile, d), lambda i: (i, 0)),
            pl.BlockSpec((1, d), lambda i: (0, 0)),
            pl.BlockSpec((None, d, n_main), lambda i: (layer, 0, 0), pipeline_mode=pl.Buffered(1)),
            pl.BlockSpec((gate_rows, d), lambda i: (0, 0)),
        ],
        out_specs=[
            pl.BlockSpec((row_tile, n_main), lambda i: (i, 0)),
            pl.BlockSpec((gate_rows, row_tile), lambda i: (0, i)),
        ],
        out_shape=[
            jax.ShapeDtypeStruct((m, n_main), BF16),
            jax.ShapeDtypeStruct((gate_rows, m), F32),
        ],
        scratch_shapes=[pltpu.VMEM((row_tile, d), BF16)],
        compiler_params=pltpu.CompilerParams(
            dimension_semantics=("arbitrary",), vmem_limit_bytes=56 * MIB),
        name="inproj",
    )(x2, norm_w, w_in_b, w_gate)


POOL_TILE = 256
POOL_HALO = 16


def _pool_body(u_ref, z_ref, pw_ref, ps_ref, o_ref):
    seq = u_ref.shape[0]
    g = pl.program_id(1)
    left = jnp.left_shift(jnp.int32(1), g)
    right = left - 1
    i = lax.broadcasted_iota(jnp.int32, (POOL_TILE, POOL_TILE), 0)
    j = lax.broadcasted_iota(jnp.int32, (POOL_TILE, POOL_TILE), 1)
    band = ((j >= i - left) & (j <= i + right)).astype(BF16)
    ih = lax.broadcasted_iota(jnp.int32, (POOL_TILE, 2 * POOL_HALO), 0)
    ph = lax.broadcasted_iota(jnp.int32, (POOL_TILE, 2 * POOL_HALO), 1)
    in_prev = (ph < POOL_HALO) & (ph - POOL_HALO >= ih - left)
    in_next = (ph >= POOL_HALO) & (ph - POOL_HALO + POOL_TILE <= ih + right)
    halo_band = (in_prev | in_next).astype(BF16)
    pw = pw_ref[...]
    ps = ps_ref[...]
    n_tiles = seq // POOL_TILE
    tok0 = lax.broadcasted_iota(jnp.int32, (POOL_TILE, u_ref.shape[1]), 0)

    def inv_count(t0):
        tok = tok0 + t0
        cnt = jnp.minimum(tok + right, seq - 1) - jnp.maximum(tok - left, 0) + 1
        return 1.0 / cnt.astype(F32)

    inv_first, inv_mid, inv_last = inv_count(0), inv_count(POOL_TILE), inv_count(seq - POOL_TILE)
    zero_halo = jnp.zeros((POOL_HALO, u_ref.shape[1]), BF16)

    def window_sums(t):
        t0 = t * POOL_TILE
        prev = u_ref[t0 - POOL_HALO:t0, :] if t > 0 else zero_halo
        nxt = u_ref[t0 + POOL_TILE:t0 + POOL_TILE + POOL_HALO, :] if t < n_tiles - 1 else zero_halo
        return _dot(band, u_ref[t0:t0 + POOL_TILE, :]) + _dot(halo_band, jnp.concatenate([prev, nxt], axis=0))

    wsum = window_sums(0)
    for t in range(n_tiles):
        t0 = t * POOL_TILE
        wsum_next = window_sums(t + 1) if t < n_tiles - 1 else None
        inv = inv_first if t == 0 else (inv_last if t == n_tiles - 1 else inv_mid)
        m = wsum * inv - u_ref[t0:t0 + POOL_TILE, :].astype(F32)
        y = _dot(m.astype(BF16), pw) * ps
        o_ref[t0:t0 + POOL_TILE, :] = (y * _silu(z_ref[t0:t0 + POOL_TILE, :].astype(F32))).astype(BF16)
        wsum = wsum_next


def _pool(proj, pool_w, pool_scale, *, batch, seq):
    pool_g = pool_w.shape[-1]
    return pl.pallas_call(
        _pool_body,
        grid=(batch, N_POOL_GROUPS),
        in_specs=[
            pl.BlockSpec((seq, pool_g), lambda b, g: (b, g)),
            pl.BlockSpec((seq, pool_g), lambda b, g: (b, N_POOL_GROUPS + g)),
            pl.BlockSpec((None, pool_g, pool_g), lambda b, g: (g, 0, 0)),
            pl.BlockSpec((None, 1, pool_g), lambda b, g: (g, 0, 0)),
        ],
        out_specs=pl.BlockSpec((seq, pool_g), lambda b, g: (b, g)),
        out_shape=jax.ShapeDtypeStruct((batch * seq, N_POOL_GROUPS * pool_g), BF16),
        compiler_params=pltpu.CompilerParams(
            dimension_semantics=("arbitrary", "arbitrary"), vmem_limit_bytes=40 * MIB),
        name="pool",
    )(proj, proj, pool_w, pool_scale)


CONV_TILE = 512
CONV_PAD = 16
REP_TILES_PER_TASK = 8


def _seg_scan(x, pos, *, suffix):
    n = x.shape[1]
    s = 1
    while s < CHUNK:
        if suffix:
            x = x + jnp.where(pos < CHUNK - s, pltpu.roll(x, n - s, axis=1), 0.0)
        else:
            x = x + jnp.where(pos >= s, pltpu.roll(x, s, axis=1), 0.0)
        s *= 2
    return x


def _dn_body(alog_ref, dtb_ref,
             q0_ref, k0_ref, v0_ref, gl0_ref, cwq0_ref, cwk0_ref, cwv0_ref,
             qn_ref, kn_ref, vn_ref, gln_ref, cwqn_ref, cwkn_ref, cwvn_ref,
             z_ref, hnw_ref,
             o_ref,
             xpad, qh, kh, vh, kt, gc_row, er_row, beta_row, rep, aq_scr, b_scr, s_scr, o_acc,
             *, n_heads, n_steps):
    seq = z_ref.shape[0]
    n_chunks = seq // CHUNK
    n_groups = seq // GROUP
    n_trips = n_groups // GROUP_UNROLL
    chunks_per_trip = GROUP_UNROLL * CHUNKS_PER_GROUP
    step = pl.program_id(0) * n_heads + pl.program_id(1)
    slot = lax.rem(step, 2)
    next_head = lax.rem(jnp.minimum(step + 1, n_steps - 1), n_heads)

    def prep_tasks(q_ref, k_ref, v_ref, gl_ref, cw_refs, head, w):
        tasks = []

        def zero_pads():
            xpad[0:CONV_PAD, :] = jnp.zeros((CONV_PAD, HEAD_DIM), F32)
            xpad[CONV_PAD + seq:CONV_PAD + seq + CONV_PAD, :] = jnp.zeros((CONV_PAD, HEAD_DIM), F32)

        tasks.append(zero_pads)
        for which, (src, cw_ref) in enumerate(zip((q_ref, k_ref, v_ref), cw_refs)):
            def fill(src=src):
                for t in range(seq // CONV_TILE):
                    t0 = t * CONV_TILE
                    xpad[CONV_PAD + t0:CONV_PAD + t0 + CONV_TILE, :] = src[t0:t0 + CONV_TILE, :].astype(F32)

            tasks.append(fill)
            for t in range(seq // CONV_TILE):
                def conv(which=which, cw_ref=cw_ref, t0=t * CONV_TILE):
                    acc = jnp.zeros((CONV_TILE, HEAD_DIM), F32)
                    for tap in range(CONV_K):
                        lo = CONV_PAD + t0 + tap - CONV_K // 2
                        acc = acc + cw_ref[tap:tap + 1, :] * xpad[lo:lo + CONV_TILE, :]
                    s = _silu(acc)
                    rows = slice(t0, t0 + CONV_TILE)
                    if which == 0:
                        s = s * (lax.rsqrt(jnp.sum(s * s, axis=-1, keepdims=True) + EPS) * (HEAD_DIM ** -0.5))
                        qh[w, rows, :] = s.astype(BF16)
                    elif which == 1:
                        s = s * lax.rsqrt(jnp.sum(s * s, axis=-1, keepdims=True) + EPS)
                        kh[w, rows, :] = s.astype(BF16)
                        kt[w, :, rows] = s.T.astype(BF16)
                    else:
                        vh[w, rows, :] = s.astype(BF16)

                tasks.append(conv)

        def gates():
            r = gl_ref[...]
            rowi = lax.broadcasted_iota(jnp.int32, r.shape, 0)
            pos = lax.broadcasted_iota(jnp.int32, r.shape, 1) & (CHUNK - 1)
            fwd_row = rowi == 0
            dtb = jnp.where(fwd_row, dtb_ref[0, head], dtb_ref[1, head])
            a_coef = jnp.exp(jnp.where(fwd_row, alog_ref[0, head], alog_ref[1, head]))
            g = jnp.where(rowi < 2, -a_coef * jax.nn.softplus(r + dtb), 0.0)
            pre = _seg_scan(g, pos, suffix=False)
            suf = _seg_scan(g, pos, suffix=True)
            gc = jnp.where(fwd_row, pre, suf)
            tot = pre + suf - g
            gc_row[w] = gc
            er_row[w] = jnp.exp(tot - gc)
            beta_row[w] = _sigmoid(r)

        tasks.append(gates)
        for d in range(2):
            for t8 in range(0, seq // LANES, REP_TILES_PER_TASK):
                def replicate(d=d, t8=t8):
                    for t in range(t8, t8 + REP_TILES_PER_TASK):
                        cols = slice(t * LANES, (t + 1) * LANES)
                        tile = jnp.broadcast_to(gc_row[w, d:d + 1, cols], (LANES, LANES))
                        rep[w, d, cols, :] = tile.T

                tasks.append(replicate)
        return tasks

    @pl.when(step == 0)
    def _():
        for task in prep_tasks(q0_ref, k0_ref, v0_ref, gl0_ref, (cwq0_ref, cwk0_ref, cwv0_ref), 0, 0):
            task()

    next_tasks = prep_tasks(qn_ref, kn_ref, vn_ref, gln_ref, (cwqn_ref, cwkn_ref, cwvn_ref), next_head, 1 - slot)

    ci = lax.broadcasted_iota(jnp.int32, (CHUNK, GROUP), 0)
    cl = lax.broadcasted_iota(jnp.int32, (CHUNK, GROUP), 1)
    cj = cl & (CHUNK - 1)
    lane_chunk = cl // CHUNK
    eye_c = (cj == ci).astype(F32)
    first_half = lax.broadcasted_iota(jnp.int32, (CHUNK, LANES), 1) < CHUNK
    bi = lax.broadcasted_iota(jnp.int32, (GROUP, GROUP), 0)
    bj = lax.broadcasted_iota(jnp.int32, (GROUP, GROUP), 1)
    bd_mask = ((bi // CHUNK) == (bj // CHUNK)).astype(F32).astype(BF16)

    def compact(full):
        out = full[(CHUNKS_PER_GROUP - 1) * CHUNK:, :]
        for c in range(CHUNKS_PER_GROUP - 2, -1, -1):
            out = jnp.where(lane_chunk == c, full[c * CHUNK:(c + 1) * CHUNK, :], out)
        return out

    def per_chunk_lanes(tok):
        blk = [tok[c * CHUNK:(c + 1) * CHUNK, :] for c in range(CHUNKS_PER_GROUP)]
        return jnp.concatenate([jnp.where(first_half, blk[0], blk[1]), jnp.where(first_half, blk[2], blk[3])], axis=1)

    def block_diag(cmp):
        cb = cmp.astype(BF16)
        return jnp.concatenate([cb] * CHUNKS_PER_GROUP, axis=0) * bd_mask

    def chunk_lanes(x, c):
        half = x[:, (c // 2) * LANES:(c // 2 + 1) * LANES]
        if c % 2:
            half = pltpu.roll(half, CHUNK, axis=1)
        return half[:, :CHUNK]

    o_acc[...] = jnp.zeros(o_acc.shape, F32)

    def recurrence_step(states, k, trip):
        for d in range(2):
            lin = trip * chunks_per_trip + k
            c = lin if d == 0 else n_chunks - 1 - lin
            r0 = c * CHUNK
            if not isinstance(r0, int):
                r0 = pl.multiple_of(r0, CHUNK)
            res = _dot(aq_scr[trip % 2, d, k], states[d].astype(BF16))
            last = r0 + (CHUNK - 1 if d == 0 else 0)
            dec = jnp.exp(rep[slot, d, pl.ds(last, 1), :])
            states[d] = states[d] * dec - res[:HEAD_DIM] + b_scr[trip % 2, d, k]
            o_acc[pl.ds(r0, CHUNK), :] += res[HEAD_DIM:]

    def precompute(trip, tick):
        chains = []
        for d in range(2):
            incl = (cj <= ci) if d == 0 else (cj >= ci)
            strict = (cj < ci) if d == 0 else (cj > ci)
            for k in range(GROUP_UNROLL):
                lin = trip * GROUP_UNROLL + k
                gi = lin if d == 0 else n_groups - 1 - lin
                rows = slice(gi * GROUP, (gi + 1) * GROUP)
                kt_g = kt[slot, :, rows]
                kk = compact(_dot(kh[slot, rows, :], kt_g))
                tick()
                qk = compact(_dot(qh[slot, rows, :], kt_g))
                tick()
                diff = per_chunk_lanes(rep[slot, d, rows, :]) - gc_row[slot, d:d + 1, rows]
                decay = jnp.where(incl, jnp.exp(jnp.where(incl, diff, 0.0)), 0.0)
                decay_beta = decay * beta_row[slot, 2 + d:3 + d, rows]
                mk = jnp.where(strict, -(kk * decay_beta), 0.0)
                a = jnp.where(incl, qk * decay_beta, 0.0)
                chains.append(dict(k=k, r0=gi * GROUP, rows=rows, d=d, mk=mk, tinv=eye_c + mk, a=a))
        for ch in chains:
            ch["mk"] = _dot(ch["mk"].astype(BF16), block_diag(ch["mk"]))
            tick()
        p = 2
        while p < CHUNK // 2:
            for ch in chains:
                x2 = _dot(jnp.concatenate([ch["tinv"], ch["mk"]], axis=0).astype(BF16), block_diag(ch["mk"]))
                tick()
                ch["tinv"] = ch["tinv"] + x2[:CHUNK]
                ch["mk"] = x2[CHUNK:]
            p *= 2
        for ch in chains:
            ch["tinv"] = ch["tinv"] + _dot(ch["tinv"].astype(BF16), block_diag(ch["mk"]))
            tick()
        for ch in chains:
            d, rows = ch["d"], ch["rows"]
            ke = (kh[slot, rows, :].astype(F32) * jnp.exp(rep[slot, d, rows, :])).astype(BF16)
            rhs = jnp.concatenate([ke, vh[slot, rows, :]], axis=1)
            ch["wu"] = _dot(block_diag(ch["tinv"]), rhs).astype(BF16)
            tick()
        for ch in chains:
            d, rows = ch["d"], ch["rows"]
            q_til = qh[slot, rows, :].astype(F32) * jnp.exp(rep[slot, d, rows, :])
            kt_til = kt[slot, :, rows].astype(F32) * (er_row[slot, d:d + 1, rows] * beta_row[slot, 2 + d:3 + d, rows])
            lhs_all = jnp.concatenate([kt_til, ch["a"]], axis=0)
            for c in range(CHUNKS_PER_GROUP):
                crow = slice(c * CHUNK, (c + 1) * CHUNK)
                res = _dot(chunk_lanes(lhs_all, c).astype(BF16), ch["wu"][crow, :])
                tick()
                loc = ch["k"] * CHUNKS_PER_GROUP + (c if d == 0 else CHUNKS_PER_GROUP - 1 - c)
                aq_scr[trip % 2, d, loc, 0:HEAD_DIM, :] = res[:HEAD_DIM, :HEAD_DIM].astype(BF16)
                aq_scr[trip % 2, d, loc, HEAD_DIM:, :] = (q_til[crow, :] - res[HEAD_DIM:, :HEAD_DIM]).astype(BF16)
                b_scr[trip % 2, d, loc] = res[:HEAD_DIM, HEAD_DIM:]
                o_acc[ch["r0"] + c * CHUNK:ch["r0"] + (c + 1) * CHUNK, :] += res[HEAD_DIM:, HEAD_DIM:]

    states = [jnp.zeros((HEAD_DIM, HEAD_DIM), F32), jnp.zeros((HEAD_DIM, HEAD_DIM), F32)]
    progress = dict(dots=0, prep=0)
    for trip in range(n_trips):
        progress["steps"] = 0 if trip > 0 else chunks_per_trip

        def tick(trip=trip):
            progress["dots"] += 1
            if progress["dots"] % REC_EVERY == 0 and progress["steps"] < chunks_per_trip:
                recurrence_step(states, progress["steps"], trip - 1)
                progress["steps"] += 1
            if progress["dots"] % PREP_EVERY == 0 and progress["prep"] < len(next_tasks):
                next_tasks[progress["prep"]]()
                progress["prep"] += 1

        precompute(trip, tick)
        while progress["steps"] < chunks_per_trip:
            recurrence_step(states, progress["steps"], trip - 1)
            progress["steps"] += 1
    for task in next_tasks[progress["prep"]:]:
        task()
    s_scr[0] = states[0]
    s_scr[1] = states[1]

    hnw = hnw_ref[...]

    def finish_rows(r0, n):
        rows = pl.ds(r0, n)
        o = o_acc[rows, :]
        y = o * lax.rsqrt(jnp.mean(o * o, axis=-1, keepdims=True) + EPS) * hnw
        o_ref[rows, :] = (y * _silu(z_ref[rows, :].astype(F32))).astype(BF16)

    def drain_step(k, carry):
        drain_states = [s_scr[0], s_scr[1]]
        recurrence_step(drain_states, k, n_trips - 1)
        s_scr[0] = drain_states[0]
        s_scr[1] = drain_states[1]
        return carry

    lax.fori_loop(0, chunks_per_trip, drain_step, 0)
    for t in range(seq // CONV_TILE):
        finish_rows(t * CONV_TILE, CONV_TILE)


def _deltanet(proj, gate_rows, conv_w3, a_log, dt_bias, head_norm_w, *, batch, seq, n_heads, qkv_block0, z_block0):
    n_steps = batch * n_heads
    chunks_per_trip = GROUP_UNROLL * CHUNKS_PER_GROUP
    smem = pl.BlockSpec(memory_space=pltpu.SMEM)

    def next_step(b, h):
        lin = jnp.minimum(b * n_heads + h + 1, n_steps - 1)
        return lin // n_heads, lin % n_heads

    def first_cols(off):
        return pl.BlockSpec((seq, HEAD_DIM), lambda b, h: (0, off), pipeline_mode=pl.Buffered(1))

    def next_cols(off):
        def index(b, h):
            nb, nh = next_step(b, h)
            return nb, off + nh
        return pl.BlockSpec((seq, HEAD_DIM), index)

    def next_gates(b, h):
        nb, nh = next_step(b, h)
        return nh, 0, nb

    def next_conv(off):
        return pl.BlockSpec((None, 8, HEAD_DIM), lambda b, h: (off + next_step(b, h)[1], 0, 0))

    first_conv = lambda off: pl.BlockSpec((None, 8, HEAD_DIM), lambda b, h: (off, 0, 0))
    k_off, v_off = qkv_block0 + n_heads, qkv_block0 + 2 * n_heads
    return pl.pallas_call(
        functools.partial(_dn_body, n_heads=n_heads, n_steps=n_steps),
        grid=(batch, n_heads),
        in_specs=[
            smem, smem,
            first_cols(qkv_block0), first_cols(k_off), first_cols(v_off),
            pl.BlockSpec((None, GATE_ROWS, seq), lambda b, h: (0, 0, 0)),
            first_conv(0), first_conv(n_heads), first_conv(2 * n_heads),
            next_cols(qkv_block0), next_cols(k_off), next_cols(v_off),
            pl.BlockSpec((None, GATE_ROWS, seq), next_gates),
            next_conv(0), next_conv(n_heads), next_conv(2 * n_heads),
            pl.BlockSpec((seq, HEAD_DIM), lambda b, h: (b, z_block0 + h)),
            pl.BlockSpec((1, HEAD_DIM), lambda b, h: (0, 0)),
        ],
        out_specs=pl.BlockSpec((seq, HEAD_DIM), lambda b, h: (b, h)),
        out_shape=jax.ShapeDtypeStruct((batch * seq, n_heads * HEAD_DIM), BF16),
        scratch_shapes=[
            pltpu.VMEM((seq + 2 * CONV_PAD, HEAD_DIM), F32),
            pltpu.VMEM((2, seq, HEAD_DIM), BF16),
            pltpu.VMEM((2, seq, HEAD_DIM), BF16),
            pltpu.VMEM((2, seq, HEAD_DIM), BF16),
            pltpu.VMEM((2, HEAD_DIM, seq), BF16),
            pltpu.VMEM((2, GATE_ROWS, seq), F32),
            pltpu.VMEM((2, GATE_ROWS, seq), F32),
            pltpu.VMEM((2, GATE_ROWS, seq), F32),
            pltpu.VMEM((2, 2, seq, LANES), F32),
            pltpu.VMEM((2, 2, chunks_per_trip, HEAD_DIM + CHUNK, HEAD_DIM), BF16),
            pltpu.VMEM((2, 2, chunks_per_trip, HEAD_DIM, HEAD_DIM), F32),
            pltpu.VMEM((2, HEAD_DIM, HEAD_DIM), F32),
            pltpu.VMEM((seq, HEAD_DIM), F32),
        ],
        compiler_params=pltpu.CompilerParams(
            dimension_semantics=("arbitrary", "arbitrary"), vmem_limit_bytes=62 * MIB),
        name="deltanet",
    )(a_log, dt_bias,
      proj, proj, proj, gate_rows, conv_w3, conv_w3, conv_w3,
      proj, proj, proj, gate_rows, conv_w3, conv_w3, conv_w3,
      proj, head_norm_w)


def _outproj_body(yp_ref, yd_ref, x_ref, w1_ref, w2_ref, fnw_ref, o_ref, *, final):
    acc = x_ref[...] + _dot(yp_ref[...], w1_ref[...].astype(BF16)) + _dot(yd_ref[...], w2_ref[...].astype(BF16))
    if final:
        acc = acc * lax.rsqrt(jnp.mean(acc * acc, axis=-1, keepdims=True) + EPS) * fnw_ref[...]
    o_ref[...] = acc


def _outproj(y_pool, y_dn, x2, w_out, layer, final_norm_w, *, final, row_tile=1024):
    m, d = x2.shape
    k1, k2 = y_pool.shape[1], y_dn.shape[1]
    assert k1 == k2
    return pl.pallas_call(
        functools.partial(_outproj_body, final=final),
        grid=(m // row_tile,),
        in_specs=[
            pl.BlockSpec((row_tile, k1), lambda i: (i, 0)),
            pl.BlockSpec((row_tile, k2), lambda i: (i, 0)),
            pl.BlockSpec((row_tile, d), lambda i: (i, 0)),
            pl.BlockSpec((None, k1, d), lambda i: (layer, 0, 0), pipeline_mode=pl.Buffered(1)),
            pl.BlockSpec((None, k2, d), lambda i: (layer, 1, 0), pipeline_mode=pl.Buffered(1)),
            pl.BlockSpec((1, d), lambda i: (0, 0)),
        ],
        out_specs=pl.BlockSpec((row_tile, d), lambda i: (i, 0)),
        out_shape=jax.ShapeDtypeStruct((m, d), F32),
        compiler_params=pltpu.CompilerParams(
            dimension_semantics=("arbitrary",), vmem_limit_bytes=40 * MIB),
        name="outproj",
    )(y_pool, y_dn, x2, w_out, w_out, final_norm_w)


def kernel(x, norm_w, w_in, pool_w, pool_scale, conv_w, a_log, dt_bias, head_norm_w, w_out, final_norm_w):
    batch, seq, d_model = x.shape
    depth = norm_w.shape[0]
    pool_g = pool_w.shape[-1]
    pool_width = N_POOL_GROUPS * pool_g
    n_dirs, n_heads = a_log.shape[1], a_log.shape[2]
    dn_width = n_heads * HEAD_DIM
    n_main = 2 * pool_width + 4 * dn_width
    n_gate = 2 * n_dirs * n_heads
    qkv_block0 = 2 * pool_width // HEAD_DIM
    z_block0 = qkv_block0 + 3 * n_heads

    x2 = x.reshape(batch * seq, d_model)
    w_in_b = w_in.astype(BF16)
    for l in range(depth):
        w_gate = w_in[l, :, n_main:n_main + n_gate].reshape(d_model, 2, n_dirs, n_heads)[:, ::-1]
        w_gate = jnp.transpose(w_gate, (0, 3, 1, 2)).reshape(d_model, n_heads, 2 * n_dirs)
        w_gate = jnp.pad(w_gate, ((0, 0), (0, 0), (0, GATE_ROWS - 2 * n_dirs))).reshape(d_model, n_heads * GATE_ROWS)
        proj, gate_t = _inproj(x2, norm_w[l][None, :], w_in_b, l, w_gate.T.astype(BF16),
                               n_main=n_main, gate_rows=n_heads * GATE_ROWS)
        gate_rows = gate_t.reshape(n_heads, GATE_ROWS, batch * seq)
        conv_w3 = jnp.pad(conv_w[l], ((0, 8 - CONV_K), (0, 0))).reshape(8, 3 * n_heads, HEAD_DIM)
        conv_w3 = jnp.transpose(conv_w3, (1, 0, 2))
        y_pool = _pool(proj, pool_w[l].astype(BF16), pool_scale[l].reshape(N_POOL_GROUPS, 1, pool_g),
                       batch=batch, seq=seq)
        y_dn = _deltanet(proj, gate_rows, conv_w3, a_log[l], dt_bias[l], head_norm_w[l][None, :],
                         batch=batch, seq=seq, n_heads=n_heads, qkv_block0=qkv_block0, z_block0=z_block0)
        x2 = _outproj(y_pool, y_dn, x2, w_out, l, final_norm_w[None, :], final=(l == depth - 1))
    return x2.reshape(batch, seq, d_model)
```
